```python
import jax, jax.numpy as jnp
from jax import lax
import numpy as np

D_MODEL = 1024
BATCH = 4
SEQ = 8192
DEPTH = 1
DEC_BATCH = 128
DEC_SEQ = 8
PAST_LEN = 8192
PAGE_SIZE = 128

MIX_DIM = D_MODEL
C_CONV = MIX_DIM // 2
CONV_GROUPS = 8
CONV_WIDTH = 31
N_HEADS = 8
HEAD_DIM = (MIX_DIM // 2) // N_HEADS
ATTN_DIM = N_HEADS * HEAD_DIM
H_IDX = 8
D_IDX = 64
TOPK_MAX = 256
D_FF = 4 * D_MODEL
ROPE_THETA = 10000.0
Q_BLOCK = 128
EPS = 1e-6
SPLIT_SIZES = (C_CONV, C_CONV, ATTN_DIM, ATTN_DIM, ATTN_DIM, H_IDX * D_IDX, D_IDX, H_IDX)
D_IN = sum(SPLIT_SIZES)

kernel_name = "hymba_conformer_dsa_decoder_step"


def rmsnorm(x, g):
    xf = x.astype(jnp.float32)
    r = lax.rsqrt(jnp.mean(xf * xf, axis=-1, keepdims=True) + EPS)
    return (xf * r).astype(x.dtype) * g


def rope(x, pos):
    half = x.shape[-1] // 2
    inv = ROPE_THETA ** (-jnp.arange(half, dtype=jnp.float32) / half)
    ang = pos.astype(jnp.float32)[:, None] * inv[None, :]
    cos = jnp.cos(ang)[None, :, None, :].astype(x.dtype)
    sin = jnp.sin(ang)[None, :, None, :].astype(x.dtype)
    x1, x2 = x[..., :half], x[..., half:]
    return jnp.concatenate([x1 * cos - x2 * sin, x1 * sin + x2 * cos], axis=-1)


def conv_module(a, gate, left, w_dw, b_dw, ln_g, ln_b):
    u = a * jax.nn.sigmoid(gate)
    ext = jnp.concatenate([left.astype(u.dtype), u], axis=1)
    y = lax.conv_general_dilated(ext, w_dw[:, None, :], window_strides=(1,), padding='VALID',
                                 dimension_numbers=('NWC', 'WIO', 'NWC'),
                                 feature_group_count=C_CONV) + b_dw
    yf = y.astype(jnp.float32)
    mu = jnp.mean(yf, axis=-1, keepdims=True)
    var = jnp.mean(jnp.square(yf - mu), axis=-1, keepdims=True)
    y = ((yf - mu) * lax.rsqrt(var + EPS)).astype(y.dtype) * ln_g + ln_b
    y = y * jax.nn.sigmoid(y)
    return y, ext[:, ext.shape[1] - (CONV_WIDTH - 1):]


def indexer_topk(q_idx, w_idx, k_idx, q_pos, topk):
    L = k_idx.shape[1]
    dots = jnp.einsum('bqhd,bld->bqhl', q_idx.astype(jnp.float32), k_idx.astype(jnp.float32)) * (D_IDX ** -0.5)
    score = jnp.einsum('bqh,bqhl->bql', w_idx.astype(jnp.float32), jax.nn.relu(dots))
    admissible = jnp.arange(L, dtype=jnp.int32)[None, :] <= q_pos[:, None]
    score = jnp.where(admissible[None], score, -jnp.inf)
    _, idx = lax.top_k(score, topk)
    valid = idx <= q_pos[None, :, None]
    return idx, valid


def sparse_softmax_attend(q, k_sel, v_sel, valid):
    logits = jnp.einsum('bqhd,bqkhd->bqhk', q, k_sel).astype(jnp.float32) * (HEAD_DIM ** -0.5)
    logits = jnp.where(valid[:, :, None, :], logits, -jnp.inf)
    p = jax.nn.softmax(logits, axis=-1).astype(v_sel.dtype)
    return jnp.einsum('bqhk,bqkhd->bqhd', p, v_sel)


def take_rows(src, idx):
    return jax.vmap(lambda s, i: s[i])(src, idx)


def prompt_attention(q, k, v, qi, ki, wi):
    B, S = q.shape[:2]
    nb = S // Q_BLOCK
    topk = min(TOPK_MAX, S // 4)

    def blk(args):
        bi, qb, qib, wib = args
        pos = bi * Q_BLOCK + jnp.arange(Q_BLOCK, dtype=jnp.int32)
        idx, valid = indexer_topk(qib, wib, ki, pos, topk)
        return sparse_softmax_attend(qb, take_rows(k, idx), take_rows(v, idx), valid)

    to_blocks = lambda t: t.reshape((B, nb, Q_BLOCK) + t.shape[2:]).swapaxes(0, 1)
    out = lax.map(blk, (jnp.arange(nb, dtype=jnp.int32), to_blocks(q), to_blocks(qi), to_blocks(wi)))
    return out.swapaxes(0, 1).reshape(B, S, ATTN_DIM)


def sample_attention(q, k, v, qi, ki, wi, cache_k, cache_v, cache_kidx, page_table):
    DB, DS = q.shape[:2]
    L = PAST_LEN + DS
    topk = min(TOPK_MAX, L // 4)
    ki_past = cache_kidx[page_table].reshape(DB, PAST_LEN, D_IDX).astype(ki.dtype)
    ki_all = jnp.concatenate([ki_past, ki], axis=1)
    pos = PAST_LEN + jnp.arange(DS, dtype=jnp.int32)
    idx, valid = indexer_topk(qi, wi, ki_all, pos, topk)
    is_past = (idx < PAST_LEN)[..., None, None]
    pidx = jnp.minimum(idx, PAST_LEN - 1)
    phys = jnp.take_along_axis(page_table, (pidx // PAGE_SIZE).reshape(DB, -1), axis=1).reshape(idx.shape)
    off = pidx % PAGE_SIZE
    nidx = jnp.clip(idx - PAST_LEN, 0, DS - 1)
    k_sel = jnp.where(is_past, cache_k[phys, off].astype(k.dtype), take_rows(k, nidx))
    v_sel = jnp.where(is_past, cache_v[phys, off].astype(v.dtype), take_rows(v, nidx))
    out = sparse_softmax_attend(q, k_sel, v_sel, valid)
    return out.reshape(DB, DS, ATTN_DIM)


def hybrid_layer(x, pos, conv_left, attn_fn, g_pre, g_post, w_in, w_dw, b_dw, ln_g, ln_b,
                 w_out, g_mlp_pre, g_mlp_post, w_up, w_down):
    B, S = x.shape[:2]
    h = rmsnorm(x, g_pre)
    z = h @ w_in
    a, gate, q, k, v, qi, ki, wi = jnp.split(z, [int(c) for c in np.cumsum(SPLIT_SIZES)[:-1]], axis=-1)
    conv_out, conv_state = conv_module(a, gate, conv_left, w_dw, b_dw, ln_g, ln_b)
    q = rope(q.reshape(B, S, N_HEADS, HEAD_DIM), pos)
    k = rope(k.reshape(B, S, N_HEADS, HEAD_DIM), pos)
    v = v.reshape(B, S, N_HEADS, HEAD_DIM)
    qi = rope(qi.reshape(B, S, H_IDX, D_IDX), pos)
    ki = rope(ki[:, :, None, :], pos)[:, :, 0]
    wi = wi * (H_IDX ** -0.5)
    attn_out = attn_fn(q, k, v, qi, ki, wi)
    mix = jnp.concatenate([conv_out, attn_out], axis=-1) @ w_out
    x = x + rmsnorm(mix, g_post)
    hm = rmsnorm(x, g_mlp_pre)
    m = jnp.square(jax.nn.relu(hm @ w_up)) @ w_down
    x = x + rmsnorm(m, g_mlp_post)
    return x, k, v, ki, conv_state


def setup_inputs(seed: int = 0) -> dict:
    key = jax.random.key(seed)
    ks = jax.random.split(key, 24)
    f32 = jnp.float32
    nrm = lambda k, shape, s: jax.random.normal(k, shape, f32) * s
    n_pages = PAST_LEN // PAGE_SIZE
    n_used = DEC_BATCH * n_pages
    n_phys = n_used + max(1, n_used // 4)
    page_table = jax.random.permutation(ks[0], n_phys)[:n_used].reshape(DEC_BATCH, n_pages).astype(jnp.int32)
    return {
        "x_prompt": nrm(ks[1], (BATCH, SEQ, D_MODEL), 1.0),
        "x_sample": nrm(ks[2], (DEC_BATCH, DEC_SEQ, D_MODEL), 1.0),
        "cache_k": nrm(ks[3], (DEPTH, n_phys, PAGE_SIZE, N_HEADS, HEAD_DIM), 1.0),
        "cache_v": nrm(ks[4], (DEPTH, n_phys, PAGE_SIZE, N_HEADS, HEAD_DIM), 1.0),
        "cache_kidx": nrm(ks[5], (DEPTH, n_phys, PAGE_SIZE, D_IDX), 1.0),
        "state_conv": nrm(ks[6], (DEPTH, DEC_BATCH, CONV_WIDTH - 1, C_CONV), 0.5),
        "page_table": page_table,
        "norm_attn_pre": 1.0 + nrm(ks[7], (DEPTH, D_MODEL), 0.01),
        "norm_attn_post": 1.0 + nrm(ks[8], (DEPTH, D_MODEL), 0.01),
        "w_in": nrm(ks[9], (DEPTH, D_MODEL, D_IN), D_MODEL ** -0.5),
        "w_dw": nrm(ks[10], (DEPTH, CONV_WIDTH, C_CONV), CONV_WIDTH ** -0.5),
        "b_dw": nrm(ks[11], (DEPTH, C_CONV), 0.01),
        "conv_ln_g": 1.0 + nrm(ks[12], (DEPTH, C_CONV), 0.01),
        "conv_ln_b": nrm(ks[13], (DEPTH, C_CONV), 0.01),
        "w_out": nrm(ks[14], (DEPTH, MIX_DIM, D_MODEL), MIX_DIM ** -0.5),
        "norm_mlp_pre": 1.0 + nrm(ks[15], (DEPTH, D_MODEL), 0.01),
        "norm_mlp_post": 1.0 + nrm(ks[16], (DEPTH, D_MODEL), 0.01),
        "w_up": nrm(ks[17], (DEPTH, D_MODEL, D_FF), D_MODEL ** -0.5),
        "w_down": nrm(ks[18], (DEPTH, D_FF, D_MODEL), D_FF ** -0.5),
    }


def reference(x_prompt, x_sample, cache_k, cache_v, cache_kidx, state_conv, page_table,
              norm_attn_pre, norm_attn_post, w_in, w_dw, b_dw, conv_ln_g, conv_ln_b, w_out,
              norm_mlp_pre, norm_mlp_post, w_up, w_down):
    pos_prompt = jnp.arange(SEQ, dtype=jnp.int32)
    pos_sample = PAST_LEN + jnp.arange(x_sample.shape[1], dtype=jnp.int32)
    xp, xs = x_prompt, x_sample
    kp, vp, kip, cp, ksm, vsm, kis, csm = [], [], [], [], [], [], [], []
    for l in range(DEPTH):
        params = (norm_attn_pre[l], norm_attn_post[l], w_in[l], w_dw[l], b_dw[l], conv_ln_g[l],
                  conv_ln_b[l], w_out[l], norm_mlp_pre[l], norm_mlp_post[l], w_up[l], w_down[l])
        left0 = jnp.zeros((xp.shape[0], CONV_WIDTH - 1, C_CONV), xp.dtype)
        xp, k1, v1, ki1, c1 = hybrid_layer(xp, pos_prompt, left0, prompt_attention, *params)
        attn_s = lambda q, k, v, qi, ki, wi, l=l: sample_attention(q, k, v, qi, ki, wi, cache_k[l], cache_v[l],
                                                                     cache_kidx[l], page_table)
        xs, k2, v2, ki2, c2 = hybrid_layer(xs, pos_sample, state_conv[l], attn_s, *params)
        kp.append(k1); vp.append(v1); kip.append(ki1); cp.append(c1)
        ksm.append(k2); vsm.append(v2); kis.append(ki2); csm.append(c2)
    return (xp, xs, jnp.stack(kp), jnp.stack(vp), jnp.stack(kip), jnp.stack(cp),
            jnp.stack(ksm), jnp.stack(vsm), jnp.stack(kis), jnp.stack(csm))
```

```python
import functools

import jax
import jax.numpy as jnp
import numpy as np
from jax import lax
from jax.experimental import pallas as pl
from jax.experimental.pallas import tpu as pltpu

F32 = jnp.float32
BF16 = jnp.bfloat16
I32 = jnp.int32

D_MODEL = 1024
C_CONV = 512
CONV_WIDTH = 31
CONV_HALO = CONV_WIDTH - 1
N_HEADS = 8
HEAD_DIM = 64
ATTN_DIM = N_HEADS * HEAD_DIM
H_IDX = 8
D_IDX = 64
TOPK_MAX = 256
D_FF = 4 * D_MODEL
ROPE_THETA = 10000.0
EPS = 1e-6
PAGE_SIZE = 128

COL_A, COL_G, COL_Q, COL_K, COL_V, COL_QI, COL_KW = 0, 512, 1024, 1536, 2048, 2560, 3072
D_IN = 3144
LANES = 128
SUBLANES = 8
D_IN_PAD = COL_KW + LANES

INT_MIN = -(2 ** 31)
F32_INF_BITS = 0x7F800000
NEG_BIG = -1e30

VMEM_LIMIT = 56 * 1024 * 1024


def _cparams(n_axes):
    return pltpu.CompilerParams(dimension_semantics=("arbitrary",) * n_axes,
                                vmem_limit_bytes=VMEM_LIMIT)


def _rms(x):
    return x * lax.rsqrt(jnp.mean(x * x, axis=-1, keepdims=True) + EPS)


def _rope_slab(x, cos, sin_signed, first_half):
    outs = []
    for c in range(x.shape[1] // LANES):
        xc = x[:, c * LANES:(c + 1) * LANES]
        partner = jnp.where(first_half, pltpu.roll(xc, LANES - 32, 1), pltpu.roll(xc, 32, 1))
        outs.append(xc * cos + partner * sin_signed)
    return outs[0] if len(outs) == 1 else jnp.concatenate(outs, axis=1)


def _project(x, g, w_ref, cos, sin_signed):
    hb = (_rms(x) * g).astype(BF16)
    dot = lambda c0, n: jnp.dot(hb, w_ref[:, c0:c0 + n], preferred_element_type=F32)
    first_half = (lax.broadcasted_iota(I32, (x.shape[0], LANES), 1) % 64) < 32
    a = dot(COL_A, C_CONV)
    gate = dot(COL_G, C_CONV)
    q = _rope_slab(dot(COL_Q, ATTN_DIM), cos, sin_signed, first_half) * (HEAD_DIM ** -0.5)
    k = _rope_slab(dot(COL_K, ATTN_DIM), cos, sin_signed, first_half)
    v = dot(COL_V, ATTN_DIM)
    qi = _rope_slab(dot(COL_QI, H_IDX * D_IDX), cos, sin_signed, first_half)
    kw = dot(COL_KW, LANES)
    ki_slab = _rope_slab(kw, cos, sin_signed, first_half)
    w_eff = kw * ((H_IDX ** -0.5) * (D_IDX ** -0.5))
    return a, gate, q, k, v, qi, ki_slab, w_eff


def _glu(a, gate):
    return a * (1.0 / (1.0 + jnp.exp(-gate)))


def _ln_swish(y, ln_g, ln_b):
    mu = jnp.mean(y, axis=-1, keepdims=True)
    d = y - mu
    var = jnp.mean(d * d, axis=-1, keepdims=True)
    y = (d * lax.rsqrt(var + EPS)) * ln_g + ln_b
    return y * (1.0 / (1.0 + jnp.exp(-y)))


def _key_to_f32(c):
    c = jnp.maximum(c, -F32_INF_BITS)
    bits = jnp.where(c >= 0, c, INT_MIN - c)
    return lax.bitcast_convert_type(bits, F32)


def _bisect_threshold(count_ge, shape, topk):
    def body(b, key):
        cand = key + lax.shift_left(jnp.int32(1), jnp.int32(31) - b)
        cnt = count_ge(_key_to_f32(cand))
        return jnp.where(cnt >= topk, cand, key)
    key = lax.fori_loop(0, 32, body, jnp.full(shape, INT_MIN, I32))
    return _key_to_f32(key)


def _bisect_tie_index(count_eq_below, need, shape, nbits):
    def body(b, c):
        cand = c + lax.shift_left(jnp.int32(1), jnp.int32(nbits - 1) - b)
        return jnp.where(count_eq_below(cand) < need, cand, c)
    return lax.fori_loop(0, nbits, body, jnp.zeros(shape, I32))


def _front_prompt_kernel(x_ref, g_ref, w_ref, cos_ref, sin_ref, wdw_ref, bdw_ref, lng_ref, lnb_ref,
                         kT_out, vT_out, kiT_out, kb_out, vTb_out, qT_out, qiT_out, wT_out, kib_out,
                         conv_out, state_out, ext_ref, *, tm, tk, conv_rows):
    j = pl.program_id(1)
    a, gate, q, k, v, qi, ki_slab, w_eff = _project(x_ref[0], g_ref[...], w_ref, cos_ref[...], sin_ref[...])

    kT_out[0] = k.T
    vT = v.T
    vT_out[0] = vT
    kiT_out[0] = ki_slab.T[:D_IDX]
    kb = k.astype(BF16)
    vTb = vT.astype(BF16)
    kib = ki_slab[:, :D_IDX].astype(BF16)
    for c in range(tm // tk):
        kb_out[0, c] = kb[c * tk:(c + 1) * tk]
        vTb_out[0, c] = vTb[:, c * tk:(c + 1) * tk]
        kib_out[0, c] = kib[c * tk:(c + 1) * tk]
    qT_out[0] = q.T.astype(BF16)
    qiT_out[0] = qi.T.astype(BF16)
    wT_out[0] = w_eff.T[D_IDX:D_IDX + H_IDX]

    @pl.when(j == 0)
    def _():
        ext_ref[0:32, :] = jnp.zeros((32, C_CONV), F32)
    ext_ref[32:32 + tm, :] = _glu(a, gate)
    for r0 in range(0, tm, conv_rows):
        acc = jnp.zeros((conv_rows, C_CONV), F32)
        for t in range(CONV_WIDTH):
            lo = r0 + 32 - CONV_HALO + t
            acc = acc + wdw_ref[t:t + 1, :] * ext_ref[lo:lo + conv_rows, :]
        y = _ln_swish(acc + bdw_ref[...], lng_ref[...], lnb_ref[...])
        conv_out[0, r0:r0 + conv_rows, :] = y.astype(BF16)
    state_out[0] = ext_ref[32 + tm - CONV_HALO:32 + tm, :]
    ext_ref[0:32, :] = ext_ref[tm:tm + 32, :]


def _front_prompt(x, g_pre, w_in_b, cos, sin_signed, w_dw, b_dw, ln_g, ln_b, *, tm, tk):
    B, S, _ = x.shape
    n_c = S // tk
    row = lambda w: pl.BlockSpec((1, tm, w), lambda b, j: (b, j, 0))
    const = lambda shp: pl.BlockSpec(shp, lambda b, j: (0,) * len(shp))
    chunked = lambda d2, d3: pl.BlockSpec((1, tm // tk, d2, d3), lambda b, j: (b, j, 0, 0))
    colT = lambda r: pl.BlockSpec((1, r, tm), lambda b, j: (b, 0, j))
    out_shape = (
        jax.ShapeDtypeStruct((B, ATTN_DIM, S), F32),
        jax.ShapeDtypeStruct((B, ATTN_DIM, S), F32),
        jax.ShapeDtypeStruct((B, D_IDX, S), F32),
        jax.ShapeDtypeStruct((B, n_c, tk, ATTN_DIM), BF16),
        jax.ShapeDtypeStruct((B, n_c, ATTN_DIM, tk), BF16),
        jax.ShapeDtypeStruct((B, ATTN_DIM, S), BF16),
        jax.ShapeDtypeStruct((B, H_IDX * D_IDX, S), BF16),
        jax.ShapeDtypeStruct((B, H_IDX, S), F32),
        jax.ShapeDtypeStruct((B, n_c, tk, D_IDX), BF16),
        jax.ShapeDtypeStruct((B, S, C_CONV), BF16),
        jax.ShapeDtypeStruct((B, CONV_HALO, C_CONV), F32),
    )
    out_specs = (colT(ATTN_DIM), colT(ATTN_DIM), colT(D_IDX), chunked(tk, ATTN_DIM), chunked(ATTN_DIM, tk),
                 colT(ATTN_DIM), colT(H_IDX * D_IDX), colT(H_IDX), chunked(tk, D_IDX), row(C_CONV),
                 pl.BlockSpec((1, CONV_HALO, C_CONV), lambda b, j: (b, 0, 0)))
    in_specs = [row(D_MODEL), const((1, D_MODEL)), const((D_MODEL, D_IN_PAD)),
                pl.BlockSpec((tm, LANES), lambda b, j: (j, 0)), pl.BlockSpec((tm, LANES), lambda b, j: (j, 0)),
                const((CONV_WIDTH, C_CONV)), const((1, C_CONV)), const((1, C_CONV)), const((1, C_CONV))]
    return pl.pallas_call(
        functools.partial(_front_prompt_kernel, tm=tm, tk=tk, conv_rows=64),
        grid=(B, S // tm), in_specs=in_specs, out_specs=out_specs, out_shape=out_shape,
        scratch_shapes=[pltpu.VMEM((tm + 32, C_CONV), F32)],
        compiler_params=_cparams(2), name="front_prompt",
    )(x, g_pre, w_in_b, cos, sin_signed, w_dw, b_dw, ln_g, ln_b)


def _attn_prompt_kernel(qT_ref, qiT_ref, wT_ref, k_ref, vT_ref, ki_ref, out_ref,
                        sc_ref, qz_ref, acc_ref, m_ref, l_ref, *, tq, tk, topk, idx_bits):
    i = pl.program_id(1)
    q0 = i * tq
    n_diag = tq // tk
    n_full = i * n_diag
    n_chunks = n_full + n_diag
    row_iota = lax.broadcasted_iota(I32, (tk, tq), 0)
    qpos = q0 + lax.broadcasted_iota(I32, (tk, tq), 1)

    def score_chunk(c, causal):
        kic = ki_ref[0, c]
        s = jnp.zeros((tk, tq), F32)
        for h in range(H_IDX):
            d = jnp.dot(kic, qiT_ref[0, h * D_IDX:(h + 1) * D_IDX, :], preferred_element_type=F32)
            s = s + wT_ref[0, h:h + 1, :] * jnp.maximum(d, 0.0)
        if causal:
            s = jnp.where(c * tk + row_iota <= qpos, s, -jnp.inf)
        sc_ref[c] = s

    def full_body(c, carry):
        score_chunk(c, False)
        return carry
    lax.fori_loop(0, n_full, full_body, 0)
    for dch in range(n_diag):
        score_chunk(n_full + dch, True)

    def count(pred):
        def body(c, acc):
            m = jnp.where(pred(c, sc_ref[c]), 1.0, 0.0)
            for r in range(tk // SUBLANES):
                acc = acc + m[r * SUBLANES:(r + 1) * SUBLANES]
            return acc
        acc = lax.fori_loop(0, n_chunks, body, jnp.zeros((SUBLANES, tq), F32))
        return jnp.sum(acc, axis=0, keepdims=True)

    thr = _bisect_threshold(lambda t: count(lambda c, s: s >= t), (1, tq), topk)
    n_ge = count(lambda c, s: s >= thr)
    tied = (n_ge > topk) & (thr > -jnp.inf)

    @pl.when(jnp.max(jnp.where(tied, 1.0, 0.0)) > 0.0)
    def _():
        need = topk - count(lambda c, s: s > thr)
        last = _bisect_tie_index(
            lambda cand: count(lambda c, s: (s == thr) & (c * tk + row_iota < cand)),
            need, (1, tq), idx_bits)

        def drop(c, carry):
            s = sc_ref[c]
            sc_ref[c] = jnp.where(tied & (s == thr) & (c * tk + row_iota > last), -jnp.inf, s)
            return carry
        lax.fori_loop(0, n_chunks, drop, 0)

    zeros_half = jnp.zeros((HEAD_DIM, tq), BF16)
    for h in range(N_HEADS):
        qh = qT_ref[0, h * HEAD_DIM:(h + 1) * HEAD_DIM, :]
        qz_ref[h] = jnp.concatenate([qh, zeros_half] if h % 2 == 0 else [zeros_half, qh], axis=0)
    m_ref[...] = jnp.full((N_HEADS, tq), NEG_BIG, F32)
    l_ref[...] = jnp.zeros((N_HEADS, tq), F32)
    acc_ref[...] = jnp.zeros((ATTN_DIM, tq), F32)

    def attend_chunk(c, causal):
        sel = sc_ref[c] >= thr
        if causal:
            sel = sel & (c * tk + row_iota <= qpos)
        for h in range(N_HEADS):
            kc = k_ref[0, c, :, (h // 2) * LANES:(h // 2 + 1) * LANES]
            s = jnp.dot(kc, qz_ref[h], preferred_element_type=F32)
            s = jnp.where(sel, s, -jnp.inf)
            m_old = m_ref[h:h + 1, :]
            m_new = jnp.maximum(m_old, jnp.max(s, axis=0, keepdims=True))
            alpha = jnp.exp(m_old - m_new)
            p = jnp.exp(s - m_new)
            l_ref[h:h + 1, :] = alpha * l_ref[h:h + 1, :] + jnp.sum(p, axis=0, keepdims=True)
            m_ref[h:h + 1, :] = m_new
            rows = slice(h * HEAD_DIM, (h + 1) * HEAD_DIM)
            pv = jnp.dot(vT_ref[0, c, rows, :], p.astype(BF16), preferred_element_type=F32)
            acc_ref[rows, :] = alpha * acc_ref[rows, :] + pv

    def attend_full(c, carry):
        attend_chunk(c, False)
        return carry
    lax.fori_loop(0, n_full, attend_full, 0)
    for dch in range(n_diag):
        attend_chunk(n_full + dch, True)

    outs = []
    for h in range(N_HEADS):
        rows = slice(h * HEAD_DIM, (h + 1) * HEAD_DIM)
        outs.append(acc_ref[rows, :] / l_ref[h:h + 1, :])
    out_ref[0] = jnp.concatenate(outs, axis=0).T.astype(BF16)


def _attn_prompt(qT, qiT, wT, k4, vT4, ki4, *, tq, tk):
    B, n_c = k4.shape[:2]
    S = n_c * tk
    topk = min(TOPK_MAX, S // 4)
    whole = lambda shp: pl.BlockSpec((1,) + shp, lambda b, i: (b,) + (0,) * len(shp))
    colT = lambda r: pl.BlockSpec((1, r, tq), lambda b, i: (b, 0, i))
    return pl.pallas_call(
        functools.partial(_attn_prompt_kernel, tq=tq, tk=tk, topk=topk, idx_bits=int(S).bit_length()),
        grid=(B, S // tq),
        in_specs=[colT(ATTN_DIM), colT(H_IDX * D_IDX), colT(H_IDX),
                  whole((n_c, tk, ATTN_DIM)), whole((n_c, ATTN_DIM, tk)), whole((n_c, tk, D_IDX))],
        out_specs=pl.BlockSpec((1, tq, ATTN_DIM), lambda b, i: (b, i, 0)),
        out_shape=jax.ShapeDtypeStruct((B, S, ATTN_DIM), BF16),
        scratch_shapes=[pltpu.VMEM((n_c, tk, tq), F32),
                        pltpu.VMEM((N_HEADS, LANES, tq), BF16),
                        pltpu.VMEM((ATTN_DIM, tq), F32),
                        pltpu.VMEM((N_HEADS, tq), F32),
                        pltpu.VMEM((N_HEADS, tq), F32)],
        compiler_params=_cparams(2), name="attn_prompt",
    )(qT, qiT, wT, k4, vT4, ki4)


def _back_kernel(x_ref, c_ref, a_ref, wo_ref, gpost_ref, gmpre_ref, gmpost_ref, wup_ref, wdn_ref, y_ref,
                 *, ff_chunk):
    mix = (jnp.dot(c_ref[...], wo_ref[0:C_CONV, :], preferred_element_type=F32)
           + jnp.dot(a_ref[...], wo_ref[C_CONV:, :], preferred_element_type=F32))
    x1 = x_ref[...] + _rms(mix) * gpost_ref[...]
    hm = (_rms(x1) * gmpre_ref[...]).astype(BF16)
    m = jnp.zeros_like(x1)
    for f in range(0, D_FF, ff_chunk):
        up = jnp.dot(hm, wup_ref[:, f:f + ff_chunk], preferred_element_type=F32)
        act = jnp.square(jnp.maximum(up, 0.0)).astype(BF16)
        m = m + jnp.dot(act, wdn_ref[f:f + ff_chunk, :], preferred_element_type=F32)
    y_ref[...] = x1 + _rms(m) * gmpost_ref[...]


def _back(x2, conv2, attn2, w_out_b, g_post, g_mpre, g_mpost, w_up_b, w_dn_b, *, tm):
    R = x2.shape[0]
    row = lambda w: pl.BlockSpec((tm, w), lambda r: (r, 0))
    const = lambda shp: pl.BlockSpec(shp, lambda r: (0,) * len(shp))
    return pl.pallas_call(
        functools.partial(_back_kernel, ff_chunk=1024),
        grid=(R // tm,),
        in_specs=[row(D_MODEL), row(C_CONV), row(ATTN_DIM), const((D_MODEL, D_MODEL)),
                  const((1, D_MODEL)), const((1, D_MODEL)), const((1, D_MODEL)),
                  const((D_MODEL, D_FF)), const((D_FF, D_MODEL))],
        out_specs=row(D_MODEL), out_shape=jax.ShapeDtypeStruct((R, D_MODEL), F32),
        compiler_params=_cparams(1), name="back",
    )(x2, conv2, attn2, w_out_b, g_post, g_mpre, g_mpost, w_up_b, w_dn_b)


def _front_sample_kernel(x_ref, g_ref, w_ref, cos_ref, sin_ref, wdw_ref, bdw_ref, lng_ref, lnb_ref, st_ref,
                         k_out, v_out, ki_out, q_out, qi_out, w_out, conv_out, state_out, ext_ref, *, gs, ds):
    a, gate, q, k, v, qi, ki_slab, w_eff = _project(x_ref[...], g_ref[...], w_ref, cos_ref[...], sin_ref[...])
    k_out[...] = k
    v_out[...] = v
    ki_out[...] = ki_slab[:, :D_IDX]
    q_out[...] = q.astype(BF16)
    qi_out[...] = qi.astype(BF16)
    w_out[...] = w_eff

    ext_ref[:, 0:CONV_HALO, :] = st_ref[...]
    ext_ref[:, CONV_HALO:CONV_HALO + ds, :] = _glu(a, gate).reshape(gs, ds, C_CONV)
    acc = jnp.zeros((gs, ds, C_CONV), F32)
    for t in range(CONV_WIDTH):
        acc = acc + wdw_ref[t:t + 1, :] * ext_ref[:, t:t + ds, :]
    y = _ln_swish(acc.reshape(gs * ds, C_CONV) + bdw_ref[...], lng_ref[...], lnb_ref[...])
    conv_out[...] = y.astype(BF16)
    state_out[...] = ext_ref[:, ds:ds + CONV_HALO, :]


def _front_sample(x2, g_pre, w_in_b, cos, sin_signed, w_dw, b_dw, ln_g, ln_b, state, *, gs, ds):
    R = x2.shape[0]
    DB = R // ds
    tm = gs * ds
    row = lambda w: pl.BlockSpec((tm, w), lambda r: (r, 0))
    const = lambda shp: pl.BlockSpec(shp, lambda r: (0,) * len(shp))
    st_spec = pl.BlockSpec((gs, CONV_HALO, C_CONV), lambda r: (r, 0, 0))
    out_shape = (
        jax.ShapeDtypeStruct((R, ATTN_DIM), F32), jax.ShapeDtypeStruct((R, ATTN_DIM), F32),
        jax.ShapeDtypeStruct((R, D_IDX), F32),
        jax.ShapeDtypeStruct((R, ATTN_DIM), BF16), jax.ShapeDtypeStruct((R, H_IDX * D_IDX), BF16),
        jax.ShapeDtypeStruct((R, LANES), F32),
        jax.ShapeDtypeStruct((R, C_CONV), BF16), jax.ShapeDtypeStruct((DB, CONV_HALO, C_CONV), F32))
    out_specs = (row(ATTN_DIM), row(ATTN_DIM), row(D_IDX), row(ATTN_DIM), row(H_IDX * D_IDX), row(LANES),
                 row(C_CONV), st_spec)
    return pl.pallas_call(
        functools.partial(_front_sample_kernel, gs=gs, ds=ds),
        grid=(R // tm,),
        in_specs=[row(D_MODEL), const((1, D_MODEL)), const((D_MODEL, D_IN_PAD)),
                  const((tm, LANES)), const((tm, LANES)),
                  const((CONV_WIDTH, C_CONV)), const((1, C_CONV)), const((1, C_CONV)), const((1, C_CONV)),
                  st_spec],
        out_specs=out_specs, out_shape=out_shape,
        scratch_shapes=[pltpu.VMEM((gs, CONV_HALO + ds + 2, C_CONV), F32)],
        compiler_params=_cparams(1), name="front_sample",
    )(x2, g_pre, w_in_b, cos, sin_signed, w_dw, b_dw, ln_g, ln_b, state)


def _idx_sample_kernel(pt_ref, *refs, pg, n_pages, ds, topk, idx_bits):
    page_refs = refs[:pg]
    qi_ref, w_ref, kin_ref, sc_ref, thr_ref, wb_ref = refs[pg:]
    s_id = pl.program_id(1)
    n_steps = n_pages // pg
    lane = lax.broadcasted_iota(I32, (ds, PAGE_SIZE), 1)
    trow = lax.broadcasted_iota(I32, (ds, PAGE_SIZE), 0)

    @pl.when(s_id == 0)
    def _():
        for h in range(H_IDX):
            wb_ref[h] = jnp.broadcast_to(w_ref[0, :, h:h + 1], (ds, PAGE_SIZE))

    def scores(keys_b, keys_on_lanes):
        contract = (((1,), (0,)), ((), ())) if keys_on_lanes else (((1,), (1,)), ((), ()))
        d = lax.dot_general(qi_ref[0], keys_b, contract, preferred_element_type=F32)
        s = jnp.zeros((ds, PAGE_SIZE), F32)
        for h in range(H_IDX):
            s = s + wb_ref[h] * jnp.maximum(d[h * ds:(h + 1) * ds], 0.0)
        return s

    for p in range(pg):
        sc_ref[0, s_id * pg + p] = scores(page_refs[p][0].astype(BF16), True)

    @pl.when(s_id == n_steps - 1)
    def _():
        kin = jnp.concatenate([kin_ref[0], jnp.zeros((PAGE_SIZE - ds, D_IDX), BF16)], axis=0)
        sc_ref[0, n_pages] = jnp.where(lane <= trow, scores(kin, False), -jnp.inf)

        def count(pred):
            def body(c, acc):
                return acc + jnp.where(pred(c, sc_ref[0, c]), 1.0, 0.0)
            acc = lax.fori_loop(0, n_pages + 1, body, jnp.zeros((ds, PAGE_SIZE), F32))
            return jnp.sum(acc, axis=1, keepdims=True)

        thr = _bisect_threshold(lambda t: count(lambda c, s: s >= t), (ds, 1), topk)
        n_ge = count(lambda c, s: s >= thr)
        tied = (n_ge > topk) & (thr > -jnp.inf)

        @pl.when(jnp.max(jnp.where(tied, 1.0, 0.0)) > 0.0)
        def _():
            need = topk - count(lambda c, s: s > thr)
            last = _bisect_tie_index(
                lambda cand: count(lambda c, s: (s == thr) & (c * PAGE_SIZE + lane < cand)),
                need, (ds, 1), idx_bits)

            def drop(c, carry):
                s = sc_ref[0, c]
                sc_ref[0, c] = jnp.where(tied & (s == thr) & (c * PAGE_SIZE + lane > last), -jnp.inf, s)
                return carry
            lax.fori_loop(0, n_pages + 1, drop, 0)

        thr_ref[0] = jnp.broadcast_to(thr, (ds, PAGE_SIZE))


def _idx_sample(page_table_flat, kidx_pages, qi_st, w_s, ki_new_b, *, pg, n_pages, ds):
    DB = qi_st.shape[0]
    L = n_pages * PAGE_SIZE + ds
    topk = min(TOPK_MAX, L // 4)
    page_spec = lambda p: pl.BlockSpec((1, D_IDX, PAGE_SIZE),
                                       lambda b, s, pt: (pt[b * n_pages + s * pg + p], 0, 0))
    per_seq = lambda shp: pl.BlockSpec((1,) + shp, lambda b, s, pt: (b,) + (0,) * len(shp))
    grid_spec = pltpu.PrefetchScalarGridSpec(
        num_scalar_prefetch=1, grid=(DB, n_pages // pg),
        in_specs=[page_spec(p) for p in range(pg)]
        + [per_seq((H_IDX * ds, D_IDX)), per_seq((ds, H_IDX)), per_seq((ds, D_IDX))],
        out_specs=[per_seq((n_pages + 1, ds, PAGE_SIZE)), per_seq((ds, PAGE_SIZE))],
        scratch_shapes=[pltpu.VMEM((H_IDX, ds, PAGE_SIZE), F32)])
    return pl.pallas_call(
        functools.partial(_idx_sample_kernel, pg=pg, n_pages=n_pages, ds=ds, topk=topk,
                          idx_bits=int(L).bit_length()),
        grid_spec=grid_spec,
        out_shape=(jax.ShapeDtypeStruct((DB, n_pages + 1, ds, PAGE_SIZE), F32),
                   jax.ShapeDtypeStruct((DB, ds, PAGE_SIZE), F32)),
        compiler_params=_cparams(2), name="idx_sample",
    )(page_table_flat, *([kidx_pages] * pg), qi_st, w_s, ki_new_b)


def _attn_sample_kernel(pt_ref, *refs, pg, n_pages, ds):
    k_refs, v_refs = refs[:pg], refs[pg:2 * pg]
    q_ref, kn_ref, vn_ref, sc_ref, thr_ref, out_ref, qbd_ref, acc_ref, m_ref, l_ref = refs[2 * pg:]
    s_id = pl.program_id(1)
    n_steps = n_pages // pg
    hq = N_HEADS * ds
    head_of_row = lax.broadcasted_iota(I32, (hq, ATTN_DIM), 0) // ds
    head_of_col = lax.broadcasted_iota(I32, (hq, ATTN_DIM), 1) // HEAD_DIM

    @pl.when(s_id == 0)
    def _():
        q_rows = jnp.concatenate([q_ref[0]] * N_HEADS, axis=0)
        qbd_ref[...] = jnp.where(head_of_row == head_of_col, q_rows, jnp.zeros_like(q_rows))
        m_ref[...] = jnp.full((hq, 1), NEG_BIG, F32)
        l_ref[...] = jnp.zeros((hq, 1), F32)
        acc_ref[...] = jnp.zeros((hq, ATTN_DIM), F32)

    nn, nt = (((1,), (0,)), ((), ())), (((1,), (1,)), ((), ()))

    def attend(kb, vb, sel, keys_on_lanes):
        s = lax.dot_general(qbd_ref[...], kb, nn if keys_on_lanes else nt, preferred_element_type=F32)
        s = s + jnp.concatenate([jnp.where(sel, 0.0, -jnp.inf)] * N_HEADS, axis=0)
        m_old = m_ref[...]
        m_new = jnp.maximum(m_old, jnp.max(s, axis=1, keepdims=True))
        alpha = jnp.exp(m_old - m_new)
        p = jnp.exp(s - m_new)
        l_ref[...] = alpha * l_ref[...] + jnp.sum(p, axis=1, keepdims=True)
        m_ref[...] = m_new
        pv = lax.dot_general(p.astype(BF16), vb, nt if keys_on_lanes else nn, preferred_element_type=F32)
        acc_ref[...] = alpha * acc_ref[...] + pv

    thr = thr_ref[0]
    for p in range(pg):
        attend(k_refs[p][0].astype(BF16), v_refs[p][0].astype(BF16), sc_ref[0, s_id * pg + p] >= thr, True)

    @pl.when(s_id == n_steps - 1)
    def _():
        pad = jnp.zeros((PAGE_SIZE - ds, ATTN_DIM), BF16)
        lane = lax.broadcasted_iota(I32, (ds, PAGE_SIZE), 1)
        trow = lax.broadcasted_iota(I32, (ds, PAGE_SIZE), 0)
        sel = (sc_ref[0, n_pages] >= thr) & (lane <= trow)
        attend(jnp.concatenate([kn_ref[0], pad], axis=0), jnp.concatenate([vn_ref[0], pad], axis=0), sel, False)
        o = jnp.where(head_of_row == head_of_col, acc_ref[...] / l_ref[...], 0.0)
        out = o[0:ds]
        for h in range(1, N_HEADS):
            out = out + o[h * ds:(h + 1) * ds]
        out_ref[0] = out.astype(BF16)


def _attn_sample(page_table_flat, k_pages, v_pages, q_s, k_new_b, v_new_b, scores, thr, *, pg, n_pages, ds):
    DB = q_s.shape[0]
    page_spec = lambda p: pl.BlockSpec((1, ATTN_DIM, PAGE_SIZE),
                                       lambda b, s, pt: (pt[b * n_pages + s * pg + p], 0, 0))
    per_seq = lambda shp: pl.BlockSpec((1,) + shp, lambda b, s, pt: (b,) + (0,) * len(shp))
    hq = N_HEADS * ds
    grid_spec = pltpu.PrefetchScalarGridSpec(
        num_scalar_prefetch=1, grid=(DB, n_pages // pg),
        in_specs=[page_spec(p) for p in range(pg)] * 2
        + [per_seq((ds, ATTN_DIM))] * 3 + [per_seq((n_pages + 1, ds, PAGE_SIZE)), per_seq((ds, PAGE_SIZE))],
        out_specs=per_seq((ds, ATTN_DIM)),
        scratch_shapes=[pltpu.VMEM((hq, ATTN_DIM), BF16), pltpu.VMEM((hq, ATTN_DIM), F32),
                        pltpu.VMEM((hq, 1), F32), pltpu.VMEM((hq, 1), F32)])
    return pl.pallas_call(
        functools.partial(_attn_sample_kernel, pg=pg, n_pages=n_pages, ds=ds),
        grid_spec=grid_spec, out_shape=jax.ShapeDtypeStruct((DB, ds, ATTN_DIM), BF16),
        compiler_params=_cparams(2), name="attn_sample",
    )(page_table_flat, *([k_pages] * pg), *([v_pages] * pg), q_s, k_new_b, v_new_b, scores, thr)


def _rope_tables(pos):
    half = HEAD_DIM // 2
    inv = ROPE_THETA ** (-jnp.arange(half, dtype=F32) / half)
    ang = pos.astype(F32)[:, None] * inv[None, :]
    cos, sin = jnp.cos(ang), jnp.sin(ang)
    return jnp.tile(cos, (1, 4)), jnp.tile(jnp.concatenate([-sin, sin], axis=1), (1, 2))


def _pick(n, prefs):
    for p in prefs:
        if n % p == 0:
            return p
    raise ValueError(f"no supported tile for extent {n}")


def kernel(x_prompt, x_sample, cache_k, cache_v, cache_kidx, state_conv, page_table, norm_attn_pre,
           norm_attn_post, w_in, w_dw, b_dw, conv_ln_g, conv_ln_b, w_out, norm_mlp_pre, norm_mlp_post,
           w_up, w_down):
    B, S, _ = x_prompt.shape
    DB, DS, _ = x_sample.shape
    n_pages = page_table.shape[1]
    past_len = n_pages * PAGE_SIZE
    assert w_in.shape[0] == 1, "single-layer kernel"
    assert DS == SUBLANES, "sample rows per sequence must fill one sublane tile"

    w_in_b = jnp.pad(w_in[0], ((0, 0), (0, D_IN_PAD - D_IN))).astype(BF16)
    w_out_b, w_up_b, w_dn_b = w_out[0].astype(BF16), w_up[0].astype(BF16), w_down[0].astype(BF16)
    g_pre, g_post = norm_attn_pre, norm_attn_post
    g_mpre, g_mpost = norm_mlp_pre, norm_mlp_post

    tm = _pick(S, (256, 128))
    tk = 128
    tq = _pick(S, (256, 128))
    cos_p, sin_p = _rope_tables(jnp.arange(S, dtype=I32))
    (kT_p, vT_p, kiT_p, k4, vT4, qT, qiT, wT, ki4, conv_p, state_p) = _front_prompt(
        x_prompt, g_pre, w_in_b, cos_p, sin_p, w_dw[0], b_dw, conv_ln_g, conv_ln_b, tm=tm, tk=tk)
    attn_p = _attn_prompt(qT, qiT, wT, k4, vT4, ki4, tq=tq, tk=tk)
    y_p = _back(x_prompt.reshape(B * S, D_MODEL), conv_p.reshape(B * S, C_CONV),
                attn_p.reshape(B * S, ATTN_DIM), w_out_b, g_post, g_mpre, g_mpost, w_up_b, w_dn_b,
                tm=tm).reshape(B, S, D_MODEL)

    R = DB * DS
    gs = _pick(DB, (32, 16, 8, 4, 2, 1))
    cos_s, sin_s = _rope_tables(past_len + jnp.arange(DS, dtype=I32))
    cos_s, sin_s = jnp.tile(cos_s, (gs, 1)), jnp.tile(sin_s, (gs, 1))
    (k_s, v_s, ki_s, q_s, qi_s, w_s, conv_s, state_s) = _front_sample(
        x_sample.reshape(R, D_MODEL), g_pre, w_in_b, cos_s, sin_s, w_dw[0], b_dw, conv_ln_g, conv_ln_b,
        state_conv[0], gs=gs, ds=DS)
    pg = _pick(n_pages, (8, 4, 2, 1))
    pt_flat = page_table.reshape(-1)
    qi_st = qi_s.reshape(DB, DS, H_IDX, D_IDX).transpose(0, 2, 1, 3).reshape(DB, H_IDX * DS, D_IDX)
    w_sq = w_s[:, D_IDX:D_IDX + H_IDX].reshape(DB, DS, H_IDX)
    n_phys = cache_k.shape[1]
    pagesT = lambda c: jnp.transpose(c[0], (0, 2, 3, 1)).reshape(n_phys, ATTN_DIM, PAGE_SIZE)
    scores, thr = _idx_sample(pt_flat, jnp.transpose(cache_kidx[0], (0, 2, 1)), qi_st, w_sq,
                              ki_s.astype(BF16).reshape(DB, DS, D_IDX), pg=pg, n_pages=n_pages, ds=DS)
    attn_s = _attn_sample(pt_flat, pagesT(cache_k), pagesT(cache_v),
                          q_s.reshape(DB, DS, ATTN_DIM), k_s.astype(BF16).reshape(DB, DS, ATTN_DIM),
                          v_s.astype(BF16).reshape(DB, DS, ATTN_DIM), scores, thr,
                          pg=pg, n_pages=n_pages, ds=DS)
    y_s = _back(x_sample.reshape(R, D_MODEL), conv_s, attn_s.reshape(R, ATTN_DIM), w_out_b, g_post, g_mpre,
                g_mpost, w_up_b, w_dn_b, tm=_pick(R, (256, 128, 64, 32, 16, 8))).reshape(DB, DS, D_MODEL)

    hd = (N_HEADS, HEAD_DIM)
    unT = lambda t: jnp.transpose(t.reshape(B, N_HEADS, HEAD_DIM, S), (0, 3, 1, 2))[None]
    return (y_p, y_s,
            unT(kT_p), unT(vT_p), jnp.transpose(kiT_p, (0, 2, 1))[None], state_p[None],
            k_s.reshape(1, DB, DS, *hd), v_s.reshape(1, DB, DS, *hd), ki_s.reshape(1, DB, DS, D_IDX),
            state_s[None])
```

```python
import functools

import jax
import jax.numpy as jnp
import numpy as np
from jax import lax
from jax.experimental import pallas as pl
from jax.experimental.pallas import tpu as pltpu

F32 = jnp.float32
BF16 = jnp.bfloat16
I32 = jnp.int32

D_MODEL = 1024
C_CONV = 512
CONV_WIDTH = 31
CONV_HALO = CONV_WIDTH - 1
N_HEADS = 8
HEAD_DIM = 64
ATTN_DIM = N_HEADS * HEAD_DIM
H_IDX = 8
D_IDX = 64
TOPK_MAX = 256
D_FF = 4 * D_MODEL
ROPE_THETA = 10000.0
EPS = 1e-6
PAGE_SIZE = 128

COL_A, COL_G, COL_Q, COL_K, COL_V, COL_QI, COL_KW = 0, 512, 1024, 1536, 2048, 2560, 3072
D_IN = 3144
LANES = 128
SUBLANES = 8
D_IN_PAD = COL_KW + LANES

INT_MIN = -(2 ** 31)
F32_INF_BITS = 0x7F800000
NEG_BIG = -1e30
BF16_ROWS = 16
V_ROWS = HEAD_DIM + BF16_ROWS
LOG2_E = 1.4426950408889634

VMEM_LIMIT = 56 * 1024 * 1024


def _cparams(n_axes):
    return pltpu.CompilerParams(dimension_semantics=("arbitrary",) * n_axes,
                                vmem_limit_bytes=VMEM_LIMIT)


def _rms(x):
    return x * lax.rsqrt(jnp.mean(x * x, axis=-1, keepdims=True) + EPS)


def _rope_slab(x, cos, sin_signed, first_half):
    outs = []
    for c in range(x.shape[1] // LANES):
        xc = x[:, c * LANES:(c + 1) * LANES]
        partner = jnp.where(first_half, pltpu.roll(xc, LANES - 32, 1), pltpu.roll(xc, 32, 1))
        outs.append(xc * cos + partner * sin_signed)
    return outs[0] if len(outs) == 1 else jnp.concatenate(outs, axis=1)


def _project(x, g, w_ref, cos, sin_signed):
    hb = (_rms(x) * g).astype(BF16)
    dot = lambda c0, n: jnp.dot(hb, w_ref[:, c0:c0 + n], preferred_element_type=F32)
    first_half = (lax.broadcasted_iota(I32, (x.shape[0], LANES), 1) % 64) < 32
    a = dot(COL_A, C_CONV)
    gate = dot(COL_G, C_CONV)
    q = _rope_slab(dot(COL_Q, ATTN_DIM), cos, sin_signed, first_half) * (HEAD_DIM ** -0.5 * LOG2_E)
    k = _rope_slab(dot(COL_K, ATTN_DIM), cos, sin_signed, first_half)
    v = dot(COL_V, ATTN_DIM)
    qi = _rope_slab(dot(COL_QI, H_IDX * D_IDX), cos, sin_signed, first_half)
    kw = dot(COL_KW, LANES)
    ki_slab = _rope_slab(kw, cos, sin_signed, first_half)
    w_eff = kw * ((H_IDX ** -0.5) * (D_IDX ** -0.5))
    return a, gate, q, k, v, qi, ki_slab, w_eff


def _glu(a, gate):
    return a * (1.0 / (1.0 + jnp.exp(-gate)))


def _ln_swish(y, ln_g, ln_b):
    mu = jnp.mean(y, axis=-1, keepdims=True)
    d = y - mu
    var = jnp.mean(d * d, axis=-1, keepdims=True)
    y = (d * lax.rsqrt(var + EPS)) * ln_g + ln_b
    return y * (1.0 / (1.0 + jnp.exp(-y)))


def _key_to_f32(c):
    c = jnp.maximum(c, -F32_INF_BITS)
    bits = jnp.where(c >= 0, c, INT_MIN - c)
    return lax.bitcast_convert_type(bits, F32)


def _bisect_threshold(count_ge, shape, topk):
    def cond(st):
        return (st[0] < 32) & st[3]

    def body(st):
        b, key, cnt, _ = st
        cand = key + lax.shift_left(jnp.int32(1), jnp.int32(31) - b)
        c = count_ge(_key_to_f32(cand))
        take = c >= topk
        cnt = jnp.where(take, c, cnt)
        still_open = jnp.max(jnp.where(cnt != topk, 1.0, 0.0)) > 0.0
        return b + 1, jnp.where(take, cand, key), cnt, still_open

    init = (jnp.int32(0), jnp.full(shape, INT_MIN, I32), jnp.full(shape, jnp.inf, F32), jnp.bool_(True))
    _, key, cnt, _ = lax.while_loop(cond, body, init)
    return _key_to_f32(key), cnt


def _bisect_tie_index(count_eq_below, need, shape, nbits):
    def body(b, c):
        cand = c + lax.shift_left(jnp.int32(1), jnp.int32(nbits - 1) - b)
        return jnp.where(count_eq_below(cand) < need, cand, c)
    return lax.fori_loop(0, nbits, body, jnp.zeros(shape, I32))


def _front_prompt_kernel(x_ref, g_ref, w_ref, cos_ref, sin_ref, wdw_ref, bdw_ref, lng_ref, lnb_ref,
                         kT_out, vT_out, kiT_out, kb_out, vTb_out, qT_out, qi2_out, wT_out, kib_out,
                         conv_out, state_out, ext_ref, *, tm, tk, conv_rows):
    j = pl.program_id(1)
    a, gate, q, k, v, qi, ki_slab, w_eff = _project(x_ref[0], g_ref[...], w_ref, cos_ref[...], sin_ref[...])

    kT_out[0] = k.T
    vT = v.T
    vT_out[0] = vT
    kiT_out[0] = ki_slab.T[:D_IDX]
    kb = k.astype(BF16)
    vTb = vT.astype(BF16)
    ones = jnp.ones((V_ROWS - HEAD_DIM, tm), BF16)
    vTx = jnp.concatenate([blk for h in range(N_HEADS)
                           for blk in (vTb[h * HEAD_DIM:(h + 1) * HEAD_DIM], ones)], axis=0)
    kib = ki_slab[:, :D_IDX].astype(BF16)
    for c in range(tm // tk):
        kb_out[0, c] = kb[c * tk:(c + 1) * tk]
        vTb_out[0, c] = vTx[:, c * tk:(c + 1) * tk]
        kib_out[0, c] = kib[c * tk:(c + 1) * tk]
    qT_out[0] = q.T.astype(BF16)
    qiT = qi.T.astype(BF16)
    qi2_out[0, 0] = jnp.concatenate([qiT[h * D_IDX:(h + 1) * D_IDX] for h in range(H_IDX)], axis=1)
    wT_out[0] = w_eff.T[D_IDX:D_IDX + H_IDX]

    @pl.when(j == 0)
    def _():
        ext_ref[0:32, :] = jnp.zeros((32, C_CONV), F32)
    ext_ref[32:32 + tm, :] = _glu(a, gate)
    for r0 in range(0, tm, conv_rows):
        acc = jnp.zeros((conv_rows, C_CONV), F32)
        for t in range(CONV_WIDTH):
            lo = r0 + 32 - CONV_HALO + t
            acc = acc + wdw_ref[t:t + 1, :] * ext_ref[lo:lo + conv_rows, :]
        y = _ln_swish(acc + bdw_ref[...], lng_ref[...], lnb_ref[...])
        conv_out[0, r0:r0 + conv_rows, :] = y.astype(BF16)
    state_out[0] = ext_ref[32 + tm - CONV_HALO:32 + tm, :]
    ext_ref[0:32, :] = ext_ref[tm:tm + 32, :]


def _front_prompt(x, g_pre, w_in_b, cos, sin_signed, w_dw, b_dw, ln_g, ln_b, *, tm, tk):
    B, S, _ = x.shape
    n_c = S // tk
    row = lambda w: pl.BlockSpec((1, tm, w), lambda b, j: (b, j, 0))
    const = lambda shp: pl.BlockSpec(shp, lambda b, j: (0,) * len(shp))
    chunked = lambda d2, d3: pl.BlockSpec((1, tm // tk, d2, d3), lambda b, j: (b, j, 0, 0))
    colT = lambda r: pl.BlockSpec((1, r, tm), lambda b, j: (b, 0, j))
    out_shape = (
        jax.ShapeDtypeStruct((B, ATTN_DIM, S), F32),
        jax.ShapeDtypeStruct((B, ATTN_DIM, S), F32),
        jax.ShapeDtypeStruct((B, D_IDX, S), F32),
        jax.ShapeDtypeStruct((B, n_c, tk, ATTN_DIM), BF16),
        jax.ShapeDtypeStruct((B, n_c, N_HEADS * V_ROWS, tk), BF16),
        jax.ShapeDtypeStruct((B, ATTN_DIM, S), BF16),
        jax.ShapeDtypeStruct((B, S // tm, D_IDX, H_IDX * tm), BF16),
        jax.ShapeDtypeStruct((B, H_IDX, S), F32),
        jax.ShapeDtypeStruct((B, n_c, tk, D_IDX), BF16),
        jax.ShapeDtypeStruct((B, S, C_CONV), BF16),
        jax.ShapeDtypeStruct((B, CONV_HALO, C_CONV), F32),
    )
    out_specs = (colT(ATTN_DIM), colT(ATTN_DIM), colT(D_IDX), chunked(tk, ATTN_DIM),
                 chunked(N_HEADS * V_ROWS, tk), colT(ATTN_DIM),
                 pl.BlockSpec((1, 1, D_IDX, H_IDX * tm), lambda b, j: (b, j, 0, 0)),
                 colT(H_IDX), chunked(tk, D_IDX), row(C_CONV),
                 pl.BlockSpec((1, CONV_HALO, C_CONV), lambda b, j: (b, 0, 0)))
    in_specs = [row(D_MODEL), const((1, D_MODEL)), const((D_MODEL, D_IN_PAD)),
                pl.BlockSpec((tm, LANES), lambda b, j: (j, 0)), pl.BlockSpec((tm, LANES), lambda b, j: (j, 0)),
                const((CONV_WIDTH, C_CONV)), const((1, C_CONV)), const((1, C_CONV)), const((1, C_CONV))]
    return pl.pallas_call(
        functools.partial(_front_prompt_kernel, tm=tm, tk=tk, conv_rows=64),
        grid=(B, S // tm), in_specs=in_specs, out_specs=out_specs, out_shape=out_shape,
        scratch_shapes=[pltpu.VMEM((tm + 32, C_CONV), F32)],
        compiler_params=_cparams(2), name="front_prompt",
    )(x, g_pre, w_in_b, cos, sin_signed, w_dw, b_dw, ln_g, ln_b)


def _attn_prompt_kernel(qT_ref, qi2_ref, wT_ref, k_ref, vT_ref, ki_ref, out_ref,
                        sc_ref, qz_ref, acc_ref, m_ref, *, tq, tk, topk, idx_bits):
    i = pl.program_id(1)
    q0 = i * tq
    n_diag = tq // tk
    n_full = i * n_diag
    n_chunks = n_full + n_diag
    n_rg = tk // SUBLANES
    row_iota = lax.broadcasted_iota(I32, (tk, tq), 0)
    qpos = q0 + lax.broadcasted_iota(I32, (tk, tq), 1)

    def score_chunk(c, causal):
        d = jnp.dot(ki_ref[0, c], qi2_ref[0, 0], preferred_element_type=F32)
        s = jnp.zeros((tk, tq), F32)
        for h in range(H_IDX):
            s = s + wT_ref[0, h:h + 1, :] * jnp.maximum(d[:, h * tq:(h + 1) * tq], 0.0)
        if causal:
            s = jnp.where(c * tk + row_iota <= qpos, s, -jnp.inf)
        sc_ref[c] = s

    def full_body(c, carry):
        score_chunk(c, False)
        return carry
    lax.fori_loop(0, n_full, full_body, 0)
    for dch in range(n_diag):
        score_chunk(n_full + dch, True)

    def count(pred):
        def body(g, acc):
            for u in range(n_diag):
                c = g * n_diag + u
                m = jnp.where(pred(c, sc_ref[c]), 1.0, 0.0).reshape(n_rg, SUBLANES, tq)
                acc = acc + jnp.sum(m, axis=0)
            return acc
        acc = lax.fori_loop(0, i + 1, body, jnp.zeros((SUBLANES, tq), F32))
        return jnp.sum(acc, axis=0, keepdims=True)

    thr, n_ge = _bisect_threshold(lambda t: count(lambda c, s: s >= t), (1, tq), topk)
    tied = (n_ge > topk) & (thr > -jnp.inf)

    @pl.when(jnp.max(jnp.where(tied, 1.0, 0.0)) > 0.0)
    def _():
        need = topk - count(lambda c, s: s > thr)
        last = _bisect_tie_index(
            lambda cand: count(lambda c, s: (s == thr) & (c * tk + row_iota < cand)),
            need, (1, tq), idx_bits)

        def drop(c, carry):
            s = sc_ref[c]
            sc_ref[c] = jnp.where(tied & (s == thr) & (c * tk + row_iota > last), -jnp.inf, s)
            return carry
        lax.fori_loop(0, n_chunks, drop, 0)

    zeros_half = jnp.zeros((HEAD_DIM, tq), BF16)
    for pr in range(N_HEADS // 2):
        q_even = qT_ref[0, (2 * pr) * HEAD_DIM:(2 * pr + 1) * HEAD_DIM, :]
        q_odd = qT_ref[0, (2 * pr + 1) * HEAD_DIM:(2 * pr + 2) * HEAD_DIM, :]
        qz_ref[pr] = jnp.concatenate([jnp.concatenate([q_even, zeros_half], axis=1),
                                      jnp.concatenate([zeros_half, q_odd], axis=1)], axis=0)
    m_ref[...] = jnp.full((N_HEADS, SUBLANES, tq), NEG_BIG, F32)
    acc_ref[...] = jnp.zeros((N_HEADS * V_ROWS, tq), F32)

    def attend_chunk(c, causal):
        sel = sc_ref[c] >= thr
        if causal:
            sel = sel & (c * tk + row_iota <= qpos)
        neg = jnp.where(sel, 0.0, -jnp.inf)
        s_pairs = [jnp.dot(k_ref[0, c, :, pr * LANES:(pr + 1) * LANES], qz_ref[pr], preferred_element_type=F32)
                   for pr in range(N_HEADS // 2)]
        for h in range(N_HEADS):
            s = (s_pairs[h // 2][:, (h % 2) * tq:(h % 2 + 1) * tq] + neg).reshape(n_rg, SUBLANES, tq)
            m_old = m_ref[h]
            col_max = jnp.max(jnp.max(s, axis=0), axis=0, keepdims=True)
            m_new = jnp.maximum(m_old, col_max)
            alpha = jnp.exp2(m_old - m_new)
            p = jnp.exp2(s - m_new[None]).reshape(tk, tq).astype(BF16)
            m_ref[h] = m_new
            rows = slice(h * V_ROWS, (h + 1) * V_ROWS)
            pv = jnp.dot(vT_ref[0, c, rows, :], p, preferred_element_type=F32)
            acc = acc_ref[rows, :].reshape(V_ROWS // SUBLANES, SUBLANES, tq) * alpha[None]
            acc_ref[rows, :] = acc.reshape(V_ROWS, tq) + pv

    def attend_full(c, carry):
        attend_chunk(c, False)
        return carry
    lax.fori_loop(0, n_full, attend_full, 0)
    for dch in range(n_diag):
        attend_chunk(n_full + dch, True)

    outs = []
    for h in range(N_HEADS):
        denom = acc_ref[h * V_ROWS + HEAD_DIM:h * V_ROWS + HEAD_DIM + 1, :]
        outs.append(acc_ref[h * V_ROWS:h * V_ROWS + HEAD_DIM, :] / denom)
    out_ref[0] = jnp.concatenate(outs, axis=0).T.astype(BF16)


def _attn_prompt(qT, qi2, wT, k4, vT4, ki4, *, tq, tk):
    B, n_c = k4.shape[:2]
    S = n_c * tk
    topk = min(TOPK_MAX, S // 4)
    whole = lambda shp: pl.BlockSpec((1,) + shp, lambda b, i: (b,) + (0,) * len(shp))
    colT = lambda r: pl.BlockSpec((1, r, tq), lambda b, i: (b, 0, i))
    return pl.pallas_call(
        functools.partial(_attn_prompt_kernel, tq=tq, tk=tk, topk=topk, idx_bits=int(S).bit_length()),
        grid=(B, S // tq),
        in_specs=[colT(ATTN_DIM), pl.BlockSpec((1, 1, D_IDX, H_IDX * tq), lambda b, i: (b, i, 0, 0)), colT(H_IDX),
                  whole((n_c, tk, ATTN_DIM)), whole((n_c, N_HEADS * V_ROWS, tk)), whole((n_c, tk, D_IDX))],
        out_specs=pl.BlockSpec((1, tq, ATTN_DIM), lambda b, i: (b, i, 0)),
        out_shape=jax.ShapeDtypeStruct((B, S, ATTN_DIM), BF16),
        scratch_shapes=[pltpu.VMEM((n_c, tk, tq), F32),
                        pltpu.VMEM((N_HEADS // 2, LANES, 2 * tq), BF16),
                        pltpu.VMEM((N_HEADS * V_ROWS, tq), F32),
                        pltpu.VMEM((N_HEADS, SUBLANES, tq), F32)],
        compiler_params=_cparams(2), name="attn_prompt",
    )(qT, qi2, wT, k4, vT4, ki4)


def _back_kernel(x_ref, c_ref, a_ref, wo_ref, gpost_ref, gmpre_ref, gmpost_ref, wup_ref, wdn_ref, y_ref,
                 *, ff_chunk):
    mix = (jnp.dot(c_ref[...], wo_ref[0:C_CONV, :], preferred_element_type=F32)
           + jnp.dot(a_ref[...], wo_ref[C_CONV:, :], preferred_element_type=F32))
    x1 = x_ref[...] + _rms(mix) * gpost_ref[...]
    hm = (_rms(x1) * gmpre_ref[...]).astype(BF16)
    m = jnp.zeros_like(x1)
    for f in range(0, D_FF, ff_chunk):
        up = jnp.dot(hm, wup_ref[:, f:f + ff_chunk], preferred_element_type=F32)
        act = jnp.square(jnp.maximum(up, 0.0)).astype(BF16)
        m = m + jnp.dot(act, wdn_ref[f:f + ff_chunk, :], preferred_element_type=F32)
    y_ref[...] = x1 + _rms(m) * gmpost_ref[...]


def _back(x2, conv2, attn2, w_out_b, g_post, g_mpre, g_mpost, w_up_b, w_dn_b, *, tm):
    R = x2.shape[0]
    row = lambda w: pl.BlockSpec((tm, w), lambda r: (r, 0))
    const = lambda shp: pl.BlockSpec(shp, lambda r: (0,) * len(shp))
    return pl.pallas_call(
        functools.partial(_back_kernel, ff_chunk=1024),
        grid=(R // tm,),
        in_specs=[row(D_MODEL), row(C_CONV), row(ATTN_DIM), const((D_MODEL, D_MODEL)),
                  const((1, D_MODEL)), const((1, D_MODEL)), const((1, D_MODEL)),
                  const((D_MODEL, D_FF)), const((D_FF, D_MODEL))],
        out_specs=row(D_MODEL), out_shape=jax.ShapeDtypeStruct((R, D_MODEL), F32),
        compiler_params=_cparams(1), name="back",
    )(x2, conv2, attn2, w_out_b, g_post, g_mpre, g_mpost, w_up_b, w_dn_b)


def _front_sample_kernel(x_ref, g_ref, w_ref, cos_ref, sin_ref, wdw_ref, bdw_ref, lng_ref, lnb_ref, st_ref,
                         k_out, v_out, ki_out, q_out, qi_out, w_out, conv_out, state_out, ext_ref, *, gs, ds):
    a, gate, q, k, v, qi, ki_slab, w_eff = _project(x_ref[...], g_ref[...], w_ref, cos_ref[...], sin_ref[...])
    k_out[...] = k
    v_out[...] = v
    ki_out[...] = ki_slab[:, :D_IDX]
    q_out[...] = q.astype(BF16)
    qi_out[...] = qi.astype(BF16)
    w_out[...] = w_eff

    ext_ref[:, 0:CONV_HALO, :] = st_ref[...]
    ext_ref[:, CONV_HALO:CONV_HALO + ds, :] = _glu(a, gate).reshape(gs, ds, C_CONV)
    acc = jnp.zeros((gs, ds, C_CONV), F32)
    for t in range(CONV_WIDTH):
        acc = acc + wdw_ref[t:t + 1, :] * ext_ref[:, t:t + ds, :]
    y = _ln_swish(acc.reshape(gs * ds, C_CONV) + bdw_ref[...], lng_ref[...], lnb_ref[...])
    conv_out[...] = y.astype(BF16)
    state_out[...] = ext_ref[:, ds:ds + CONV_HALO, :]


def _front_sample(x2, g_pre, w_in_b, cos, sin_signed, w_dw, b_dw, ln_g, ln_b, state, *, gs, ds):
    R = x2.shape[0]
    DB = R // ds
    tm = gs * ds
    row = lambda w: pl.BlockSpec((tm, w), lambda r: (r, 0))
    const = lambda shp: pl.BlockSpec(shp, lambda r: (0,) * len(shp))
    st_spec = pl.BlockSpec((gs, CONV_HALO, C_CONV), lambda r: (r, 0, 0))
    out_shape = (
        jax.ShapeDtypeStruct((R, ATTN_DIM), F32), jax.ShapeDtypeStruct((R, ATTN_DIM), F32),
        jax.ShapeDtypeStruct((R, D_IDX), F32),
        jax.ShapeDtypeStruct((R, ATTN_DIM), BF16), jax.ShapeDtypeStruct((R, H_IDX * D_IDX), BF16),
        jax.ShapeDtypeStruct((R, LANES), F32),
        jax.ShapeDtypeStruct((R, C_CONV), BF16), jax.ShapeDtypeStruct((DB, CONV_HALO, C_CONV), F32))
    out_specs = (row(ATTN_DIM), row(ATTN_DIM), row(D_IDX), row(ATTN_DIM), row(H_IDX * D_IDX), row(LANES),
                 row(C_CONV), st_spec)
    return pl.pallas_call(
        functools.partial(_front_sample_kernel, gs=gs, ds=ds),
        grid=(R // tm,),
        in_specs=[row(D_MODEL), const((1, D_MODEL)), const((D_MODEL, D_IN_PAD)),
                  const((tm, LANES)), const((tm, LANES)),
                  const((CONV_WIDTH, C_CONV)), const((1, C_CONV)), const((1, C_CONV)), const((1, C_CONV)),
                  st_spec],
        out_specs=out_specs, out_shape=out_shape,
        scratch_shapes=[pltpu.VMEM((gs, CONV_HALO + ds + 2, C_CONV), F32)],
        compiler_params=_cparams(1), name="front_sample",
    )(x2, g_pre, w_in_b, cos, sin_signed, w_dw, b_dw, ln_g, ln_b, state)


def _idx_sample_kernel(pt_ref, *refs, pg, n_pages, ds, topk, idx_bits):
    page_refs = refs[:pg]
    qi_ref, w_ref, kin_ref, sc_ref, thr_ref, wb_ref = refs[pg:]
    s_id = pl.program_id(1)
    n_steps = n_pages // pg
    lane = lax.broadcasted_iota(I32, (ds, PAGE_SIZE), 1)
    trow = lax.broadcasted_iota(I32, (ds, PAGE_SIZE), 0)

    @pl.when(s_id == 0)
    def _():
        for h in range(H_IDX):
            wb_ref[h] = jnp.broadcast_to(w_ref[0, :, h:h + 1], (ds, PAGE_SIZE))

    def scores(keys_b, keys_on_lanes):
        contract = (((1,), (0,)), ((), ())) if keys_on_lanes else (((1,), (1,)), ((), ()))
        d = lax.dot_general(qi_ref[0], keys_b, contract, preferred_element_type=F32)
        s = jnp.zeros((ds, PAGE_SIZE), F32)
        for h in range(H_IDX):
            s = s + wb_ref[h] * jnp.maximum(d[h * ds:(h + 1) * ds], 0.0)
        return s

    for p in range(pg):
        sc_ref[0, s_id * pg + p] = scores(page_refs[p][0].astype(BF16), True)

    @pl.when(s_id == n_steps - 1)
    def _():
        kin = jnp.concatenate([kin_ref[0], jnp.zeros((PAGE_SIZE - ds, D_IDX), BF16)], axis=0)
        sc_ref[0, n_pages] = jnp.where(lane <= trow, scores(kin, False), -jnp.inf)

        page_iota = lax.broadcasted_iota(I32, (n_pages + 1, ds, PAGE_SIZE), 0)

        def count(pred):
            hits = jnp.where(pred(page_iota, sc_ref[0]), 1.0, 0.0)
            return jnp.sum(jnp.sum(hits, axis=0), axis=1, keepdims=True)

        thr, n_ge = _bisect_threshold(lambda t: count(lambda c, s: s >= t), (ds, 1), topk)
        tied = (n_ge > topk) & (thr > -jnp.inf)

        @pl.when(jnp.max(jnp.where(tied, 1.0, 0.0)) > 0.0)
        def _():
            need = topk - count(lambda c, s: s > thr)
            last = _bisect_tie_index(
                lambda cand: count(lambda c, s: (s == thr) & (c * PAGE_SIZE + lane < cand)),
                need, (ds, 1), idx_bits)

            def drop(c, carry):
                s = sc_ref[0, c]
                sc_ref[0, c] = jnp.where(tied & (s == thr) & (c * PAGE_SIZE + lane > last), -jnp.inf, s)
                return carry
            lax.fori_loop(0, n_pages + 1, drop, 0)

        thr_ref[0] = jnp.broadcast_to(thr, (ds, PAGE_SIZE))


def _idx_sample(page_table_flat, kidx_pages, qi_st, w_s, ki_new_b, *, pg, n_pages, ds):
    DB = qi_st.shape[0]
    L = n_pages * PAGE_SIZE + ds
    topk = min(TOPK_MAX, L // 4)
    page_spec = lambda p: pl.BlockSpec((1, D_IDX, PAGE_SIZE),
                                       lambda b, s, pt: (pt[b * n_pages + s * pg + p], 0, 0))
    per_seq = lambda shp: pl.BlockSpec((1,) + shp, lambda b, s, pt: (b,) + (0,) * len(shp))
    grid_spec = pltpu.PrefetchScalarGridSpec(
        num_scalar_prefetch=1, grid=(DB, n_pages // pg),
        in_specs=[page_spec(p) for p in range(pg)]
        + [per_seq((H_IDX * ds, D_IDX)), per_seq((ds, H_IDX)), per_seq((ds, D_IDX))],
        out_specs=[per_seq((n_pages + 1, ds, PAGE_SIZE)), per_seq((ds, PAGE_SIZE))],
        scratch_shapes=[pltpu.VMEM((H_IDX, ds, PAGE_SIZE), F32)])
    return pl.pallas_call(
        functools.partial(_idx_sample_kernel, pg=pg, n_pages=n_pages, ds=ds, topk=topk,
                          idx_bits=int(L).bit_length()),
        grid_spec=grid_spec,
        out_shape=(jax.ShapeDtypeStruct((DB, n_pages + 1, ds, PAGE_SIZE), F32),
                   jax.ShapeDtypeStruct((DB, ds, PAGE_SIZE), F32)),
        compiler_params=_cparams(2), name="idx_sample",
    )(page_table_flat, *([kidx_pages] * pg), qi_st, w_s, ki_new_b)


def _attn_sample_kernel(pt_ref, *refs, pg, n_pages, ds):
    k_refs, v_refs = refs[:pg], refs[pg:2 * pg]
    q_ref, kn_ref, vn_ref, sc_ref, thr_ref, out_ref, qbd_ref, acc_ref, m_ref, l_ref = refs[2 * pg:]
    s_id = pl.program_id(1)
    n_steps = n_pages // pg
    hq = N_HEADS * ds
    head_of_row = lax.broadcasted_iota(I32, (hq, ATTN_DIM), 0) // ds
    head_of_col = lax.broadcasted_iota(I32, (hq, ATTN_DIM), 1) // HEAD_DIM

    @pl.when(s_id == 0)
    def _():
        q_rows = jnp.concatenate([q_ref[0]] * N_HEADS, axis=0)
        qbd_ref[...] = jnp.where(head_of_row == head_of_col, q_rows, jnp.zeros_like(q_rows))
        m_ref[...] = jnp.full((hq, 1), NEG_BIG, F32)
        l_ref[...] = jnp.zeros((hq, 1), F32)
        acc_ref[...] = jnp.zeros((hq, ATTN_DIM), F32)

    nn, nt = (((1,), (0,)), ((), ())), (((1,), (1,)), ((), ()))

    def attend(kbs, vbs, sels, keys_on_lanes):
        ss = [lax.dot_general(qbd_ref[...], kb, nn if keys_on_lanes else nt, preferred_element_type=F32)
              + jnp.concatenate([jnp.where(sel, 0.0, -jnp.inf)] * N_HEADS, axis=0)
              for kb, sel in zip(kbs, sels)]
        blk_max = functools.reduce(jnp.maximum, ss)
        m_old = m_ref[...]
        m_new = jnp.maximum(m_old, jnp.max(blk_max, axis=1, keepdims=True))
        alpha = jnp.exp2(m_old - m_new)
        ps = [jnp.exp2(s - m_new) for s in ss]
        l_ref[...] = alpha * l_ref[...] + jnp.sum(functools.reduce(jnp.add, ps), axis=1, keepdims=True)
        m_ref[...] = m_new
        pv = functools.reduce(jnp.add, [
            lax.dot_general(p.astype(BF16), vb, nt if keys_on_lanes else nn, preferred_element_type=F32)
            for p, vb in zip(ps, vbs)])
        acc_ref[...] = alpha * acc_ref[...] + pv

    thr = thr_ref[0]
    attend([r[0].astype(BF16) for r in k_refs], [r[0].astype(BF16) for r in v_refs],
           [sc_ref[0, s_id * pg + p] >= thr for p in range(pg)], True)

    @pl.when(s_id == n_steps - 1)
    def _():
        pad = jnp.zeros((PAGE_SIZE - ds, ATTN_DIM), BF16)
        lane = lax.broadcasted_iota(I32, (ds, PAGE_SIZE), 1)
        trow = lax.broadcasted_iota(I32, (ds, PAGE_SIZE), 0)
        sel = (sc_ref[0, n_pages] >= thr) & (lane <= trow)
        attend([jnp.concatenate([kn_ref[0], pad], axis=0)], [jnp.concatenate([vn_ref[0], pad], axis=0)],
               [sel], False)
        o = jnp.where(head_of_row == head_of_col, acc_ref[...] / l_ref[...], 0.0)
        out = o[0:ds]
        for h in range(1, N_HEADS):
            out = out + o[h * ds:(h + 1) * ds]
        out_ref[0] = out.astype(BF16)


def _attn_sample(page_table_flat, k_pages, v_pages, q_s, k_new_b, v_new_b, scores, thr, *, pg, n_pages, ds):
    DB = q_s.shape[0]
    page_spec = lambda p: pl.BlockSpec((1, ATTN_DIM, PAGE_SIZE),
                                       lambda b, s, pt: (pt[b * n_pages + s * pg + p], 0, 0))
    per_seq = lambda shp: pl.BlockSpec((1,) + shp, lambda b, s, pt: (b,) + (0,) * len(shp))
    hq = N_HEADS * ds
    grid_spec = pltpu.PrefetchScalarGridSpec(
        num_scalar_prefetch=1, grid=(DB, n_pages // pg),
        in_specs=[page_spec(p) for p in range(pg)] * 2
        + [per_seq((ds, ATTN_DIM))] * 3 + [per_seq((n_pages + 1, ds, PAGE_SIZE)), per_seq((ds, PAGE_SIZE))],
        out_specs=per_seq((ds, ATTN_DIM)),
        scratch_shapes=[pltpu.VMEM((hq, ATTN_DIM), BF16), pltpu.VMEM((hq, ATTN_DIM), F32),
                        pltpu.VMEM((hq, 1), F32), pltpu.VMEM((hq, 1), F32)])
    return pl.pallas_call(
        functools.partial(_attn_sample_kernel, pg=pg, n_pages=n_pages, ds=ds),
        grid_spec=grid_spec, out_shape=jax.ShapeDtypeStruct((DB, ds, ATTN_DIM), BF16),
        compiler_params=_cparams(2), name="attn_sample",
    )(page_table_flat, *([k_pages] * pg), *([v_pages] * pg), q_s, k_new_b, v_new_b, scores, thr)


def _rope_tables(pos):
    half = HEAD_DIM // 2
    inv = ROPE_THETA ** (-jnp.arange(half, dtype=F32) / half)
    ang = pos.astype(F32)[:, None] * inv[None, :]
    cos, sin = jnp.cos(ang), jnp.sin(ang)
    return jnp.tile(cos, (1, 4)), jnp.tile(jnp.concatenate([-sin, sin], axis=1), (1, 2))


def _pick(n, prefs):
    for p in prefs:
        if n % p == 0:
            return p
    raise ValueError(f"no supported tile for extent {n}")


def kernel(x_prompt, x_sample, cache_k, cache_v, cache_kidx, state_conv, page_table, norm_attn_pre,
           norm_attn_post, w_in, w_dw, b_dw, conv_ln_g, conv_ln_b, w_out, norm_mlp_pre, norm_mlp_post,
           w_up, w_down):
    B, S, _ = x_prompt.shape
    DB, DS, _ = x_sample.shape
    n_pages = page_table.shape[1]
    past_len = n_pages * PAGE_SIZE
    assert w_in.shape[0] == 1, "single-layer kernel"
    assert DS == SUBLANES, "sample rows per sequence must fill one sublane tile"

    w_in_b = jnp.pad(w_in[0], ((0, 0), (0, D_IN_PAD - D_IN))).astype(BF16)
    w_out_b, w_up_b, w_dn_b = w_out[0].astype(BF16), w_up[0].astype(BF16), w_down[0].astype(BF16)
    g_pre, g_post = norm_attn_pre, norm_attn_post
    g_mpre, g_mpost = norm_mlp_pre, norm_mlp_post

    tm = _pick(S, (256, 128))
    tk = 128
    tq = _pick(S, (256, 128))
    cos_p, sin_p = _rope_tables(jnp.arange(S, dtype=I32))
    assert tm == tq, "the front kernel lays qi out per attention query tile"
    (kT_p, vT_p, kiT_p, k4, vT4, qT, qi2, wT, ki4, conv_p, state_p) = _front_prompt(
        x_prompt, g_pre, w_in_b, cos_p, sin_p, w_dw[0], b_dw, conv_ln_g, conv_ln_b, tm=tm, tk=tk)
    attn_p = _attn_prompt(qT, qi2, wT, k4, vT4, ki4, tq=tq, tk=tk)
    y_p = _back(x_prompt.reshape(B * S, D_MODEL), conv_p.reshape(B * S, C_CONV),
                attn_p.reshape(B * S, ATTN_DIM), w_out_b, g_post, g_mpre, g_mpost, w_up_b, w_dn_b,
                tm=tm).reshape(B, S, D_MODEL)

    R = DB * DS
    gs = _pick(DB, (32, 16, 8, 4, 2, 1))
    cos_s, sin_s = _rope_tables(past_len + jnp.arange(DS, dtype=I32))
    cos_s, sin_s = jnp.tile(cos_s, (gs, 1)), jnp.tile(sin_s, (gs, 1))
    (k_s, v_s, ki_s, q_s, qi_s, w_s, conv_s, state_s) = _front_sample(
        x_sample.reshape(R, D_MODEL), g_pre, w_in_b, cos_s, sin_s, w_dw[0], b_dw, conv_ln_g, conv_ln_b,
        state_conv[0], gs=gs, ds=DS)
    pg = _pick(n_pages, (16, 8, 4, 2, 1))
    pt_flat = page_table.reshape(-1)
    qi_st = qi_s.reshape(DB, DS, H_IDX, D_IDX).transpose(0, 2, 1, 3).reshape(DB, H_IDX * DS, D_IDX)
    w_sq = w_s[:, D_IDX:D_IDX + H_IDX].reshape(DB, DS, H_IDX)
    n_phys = cache_k.shape[1]
    pagesT = lambda c: jnp.transpose(c[0], (0, 2, 3, 1)).reshape(n_phys, ATTN_DIM, PAGE_SIZE)
    scores, thr = _idx_sample(pt_flat, jnp.transpose(cache_kidx[0], (0, 2, 1)), qi_st, w_sq,
                              ki_s.astype(BF16).reshape(DB, DS, D_IDX), pg=pg, n_pages=n_pages, ds=DS)
    attn_s = _attn_sample(pt_flat, pagesT(cache_k), pagesT(cache_v),
                          q_s.reshape(DB, DS, ATTN_DIM), k_s.astype(BF16).reshape(DB, DS, ATTN_DIM),
                          v_s.astype(BF16).reshape(DB, DS, ATTN_DIM), scores, thr,
                          pg=pg, n_pages=n_pages, ds=DS)
    y_s = _back(x_sample.reshape(R, D_MODEL), conv_s, attn_s.reshape(R, ATTN_DIM), w_out_b, g_post, g_mpre,
                g_mpost, w_up_b, w_dn_b, tm=_pick(R, (256, 128, 64, 32, 16, 8))).reshape(DB, DS, D_MODEL)

    hd = (N_HEADS, HEAD_DIM)
    unT = lambda t: jnp.transpose(t.reshape(B, N_HEADS, HEAD_DIM, S), (0, 3, 1, 2))[None]
    return (y_p, y_s,
            unT(kT_p), unT(vT_p), jnp.transpose(kiT_p, (0, 2, 1))[None], state_p[None],
            k_s.reshape(1, DB, DS, *hd), v_s.reshape(1, DB, DS, *hd), ki_s.reshape(1, DB, DS, D_IDX),
            state_s[None])
```

```python
import functools

import jax
import jax.numpy as jnp
import numpy as np
from jax import lax
from jax.experimental import pallas as pl
from jax.experimental.pallas import tpu as pltpu

F32 = jnp.float32
BF16 = jnp.bfloat16
I32 = jnp.int32

D_MODEL = 1024
C_CONV = 512
CONV_WIDTH = 31
CONV_HALO = CONV_WIDTH - 1
N_HEADS = 8
HEAD_DIM = 64
ATTN_DIM = N_HEADS * HEAD_DIM
H_IDX = 8
D_IDX = 64
TOPK_MAX = 256
D_FF = 4 * D_MODEL
ROPE_THETA = 10000.0
EPS = 1e-6
PAGE_SIZE = 128

COL_A, COL_G, COL_Q, COL_K, COL_V, COL_QI, COL_KW = 0, 512, 1024, 1536, 2048, 2560, 3072
D_IN = 3144
LANES = 128
SUBLANES = 8
D_IN_PAD = COL_KW + LANES

INT_MIN = -(2 ** 31)
F32_INF_BITS = 0x7F800000
NEG_BIG = -1e30
BF16_ROWS = 16
V_ROWS = HEAD_DIM + BF16_ROWS
LOG2_E = 1.4426950408889634
MAX_RAW_LOGIT = 64.0
NORM_SLACK = 1.05

VMEM_LIMIT = 56 * 1024 * 1024


def _cparams(n_axes):
    return pltpu.CompilerParams(dimension_semantics=("arbitrary",) * n_axes,
                                vmem_limit_bytes=VMEM_LIMIT)


def _rms(x):
    return x * lax.rsqrt(jnp.mean(x * x, axis=-1, keepdims=True) + EPS)


def _rope_slab(x, cos, sin_signed, first_half):
    outs = []
    for c in range(x.shape[1] // LANES):
        xc = x[:, c * LANES:(c + 1) * LANES]
        partner = jnp.where(first_half, pltpu.roll(xc, LANES - 32, 1), pltpu.roll(xc, 32, 1))
        outs.append(xc * cos + partner * sin_signed)
    return outs[0] if len(outs) == 1 else jnp.concatenate(outs, axis=1)


def _project(x, g, w_ref, cos, sin_signed):
    hb = (_rms(x) * g).astype(BF16)
    dot = lambda c0, n: jnp.dot(hb, w_ref[:, c0:c0 + n], preferred_element_type=F32)
    first_half = (lax.broadcasted_iota(I32, (x.shape[0], LANES), 1) % 64) < 32
    a = dot(COL_A, C_CONV)
    gate = dot(COL_G, C_CONV)
    q = _rope_slab(dot(COL_Q, ATTN_DIM), cos, sin_signed, first_half) * (HEAD_DIM ** -0.5 * LOG2_E)
    k = _rope_slab(dot(COL_K, ATTN_DIM), cos, sin_signed, first_half)
    v = dot(COL_V, ATTN_DIM)
    qi = _rope_slab(dot(COL_QI, H_IDX * D_IDX), cos, sin_signed, first_half)
    kw = dot(COL_KW, LANES)
    ki_slab = _rope_slab(kw, cos, sin_signed, first_half)
    w_eff = kw * ((H_IDX ** -0.5) * (D_IDX ** -0.5))
    return a, gate, q, k, v, qi, ki_slab, w_eff


def _glu(a, gate):
    return a * (1.0 / (1.0 + jnp.exp(-gate)))


def _ln_swish(y, ln_g, ln_b):
    mu = jnp.mean(y, axis=-1, keepdims=True)
    d = y - mu
    var = jnp.mean(d * d, axis=-1, keepdims=True)
    y = (d * lax.rsqrt(var + EPS)) * ln_g + ln_b
    return y * (1.0 / (1.0 + jnp.exp(-y)))


def _key_to_f32(c):
    c = jnp.maximum(c, -F32_INF_BITS)
    bits = jnp.where(c >= 0, c, INT_MIN - c)
    return lax.bitcast_convert_type(bits, F32)


def _bisect_threshold(count_ge, shape, topk):
    def cond(st):
        return (st[0] < 32) & st[3]

    def body(st):
        b, key, cnt, _ = st
        cand = key + lax.shift_left(jnp.int32(1), jnp.int32(31) - b)
        c = count_ge(_key_to_f32(cand))
        take = c >= topk
        cnt = jnp.where(take, c, cnt)
        still_open = jnp.max(jnp.where(cnt != topk, 1.0, 0.0)) > 0.0
        return b + 1, jnp.where(take, cand, key), cnt, still_open

    init = (jnp.int32(0), jnp.full(shape, INT_MIN, I32), jnp.full(shape, jnp.inf, F32), jnp.bool_(True))
    _, key, cnt, _ = lax.while_loop(cond, body, init)
    return _key_to_f32(key), cnt


def _bisect_tie_index(count_eq_below, need, shape, nbits):
    def body(b, c):
        cand = c + lax.shift_left(jnp.int32(1), jnp.int32(nbits - 1) - b)
        return jnp.where(count_eq_below(cand) < need, cand, c)
    return lax.fori_loop(0, nbits, body, jnp.zeros(shape, I32))


def _front_prompt_kernel(x_ref, g_ref, w_ref, cos_ref, sin_ref, wdw_ref, bdw_ref, lng_ref, lnb_ref,
                         kT_out, vT_out, kiT_out, kb_out, vTb_out, qT_out, qi2_out, wT_out, kib_out, kn2_out,
                         conv_out, state_out, ext_ref, sh_ref, *, tm, tk, conv_rows):
    j = pl.program_id(1)
    a, gate, q, k, v, qi, ki_slab, w_eff = _project(x_ref[0], g_ref[...], w_ref, cos_ref[...], sin_ref[...])

    kT = k.T
    kT_out[0] = kT
    kn2 = jnp.max(jnp.sum((kT * kT).reshape(N_HEADS, HEAD_DIM, tm), axis=1), axis=1, keepdims=True)
    kn2_out[0, 0] = jnp.broadcast_to(kn2, (N_HEADS, LANES))
    vT = v.T
    vT_out[0] = vT
    kiT_out[0] = ki_slab.T[:D_IDX]
    kb = k.astype(BF16)
    vTb = vT.astype(BF16)
    ones = jnp.ones((V_ROWS - HEAD_DIM, tm), BF16)
    vTx = jnp.concatenate([blk for h in range(N_HEADS)
                           for blk in (vTb[h * HEAD_DIM:(h + 1) * HEAD_DIM], ones)], axis=0)
    kib = ki_slab[:, :D_IDX].astype(BF16)
    for c in range(tm // tk):
        kb_out[0, c] = kb[c * tk:(c + 1) * tk]
        vTb_out[0, c] = vTx[:, c * tk:(c + 1) * tk]
        kib_out[0, c] = kib[c * tk:(c + 1) * tk]
    qT_out[0] = q.T.astype(BF16)
    qiT = qi.T.astype(BF16)
    qi2_out[0, 0] = jnp.concatenate([qiT[h * D_IDX:(h + 1) * D_IDX] for h in range(H_IDX)], axis=1)
    wT_out[0] = w_eff.T[D_IDX:D_IDX + H_IDX]

    @pl.when(j == 0)
    def _():
        ext_ref[0:32, :] = jnp.zeros((32, C_CONV), F32)
    ext_ref[32:32 + tm, :] = _glu(a, gate)
    for r in range(1, SUBLANES):
        sh_ref[r - 1] = ext_ref[r:r + tm + 24, :]
    for r0 in range(0, tm, conv_rows):
        acc = jnp.zeros((conv_rows, C_CONV), F32)
        for t in range(CONV_WIDTH):
            lo = r0 + 32 - CONV_HALO + t
            r = lo % SUBLANES
            src = ext_ref if r == 0 else sh_ref.at[r - 1]
            acc = acc + wdw_ref[t:t + 1, :] * src[lo - r:lo - r + conv_rows, :]
        y = _ln_swish(acc + bdw_ref[...], lng_ref[...], lnb_ref[...])
        conv_out[0, r0:r0 + conv_rows, :] = y.astype(BF16)
    state_out[0] = ext_ref[32 + tm - CONV_HALO:32 + tm, :]
    ext_ref[0:32, :] = ext_ref[tm:tm + 32, :]


def _front_prompt(x, g_pre, w_in_b, cos, sin_signed, w_dw, b_dw, ln_g, ln_b, *, tm, tk):
    B, S, _ = x.shape
    n_c = S // tk
    row = lambda w: pl.BlockSpec((1, tm, w), lambda b, j: (b, j, 0))
    const = lambda shp: pl.BlockSpec(shp, lambda b, j: (0,) * len(shp))
    chunked = lambda d2, d3: pl.BlockSpec((1, tm // tk, d2, d3), lambda b, j: (b, j, 0, 0))
    colT = lambda r: pl.BlockSpec((1, r, tm), lambda b, j: (b, 0, j))
    out_shape = (
        jax.ShapeDtypeStruct((B, ATTN_DIM, S), F32),
        jax.ShapeDtypeStruct((B, ATTN_DIM, S), F32),
        jax.ShapeDtypeStruct((B, D_IDX, S), F32),
        jax.ShapeDtypeStruct((B, n_c, tk, ATTN_DIM), BF16),
        jax.ShapeDtypeStruct((B, n_c, N_HEADS * V_ROWS, tk), BF16),
        jax.ShapeDtypeStruct((B, ATTN_DIM, S), BF16),
        jax.ShapeDtypeStruct((B, S // tm, D_IDX, H_IDX * tm), BF16),
        jax.ShapeDtypeStruct((B, H_IDX, S), F32),
        jax.ShapeDtypeStruct((B, n_c, tk, D_IDX), BF16),
        jax.ShapeDtypeStruct((B, S // tm, N_HEADS, LANES), F32),
        jax.ShapeDtypeStruct((B, S, C_CONV), BF16),
        jax.ShapeDtypeStruct((B, CONV_HALO, C_CONV), F32),
    )
    out_specs = (colT(ATTN_DIM), colT(ATTN_DIM), colT(D_IDX), chunked(tk, ATTN_DIM),
                 chunked(N_HEADS * V_ROWS, tk), colT(ATTN_DIM),
                 pl.BlockSpec((1, 1, D_IDX, H_IDX * tm), lambda b, j: (b, j, 0, 0)),
                 colT(H_IDX), chunked(tk, D_IDX),
                 pl.BlockSpec((1, 1, N_HEADS, LANES), lambda b, j: (b, j, 0, 0)), row(C_CONV),
                 pl.BlockSpec((1, CONV_HALO, C_CONV), lambda b, j: (b, 0, 0)))
    in_specs = [row(D_MODEL), const((1, D_MODEL)), const((D_MODEL, D_IN_PAD)),
                pl.BlockSpec((tm, LANES), lambda b, j: (j, 0)), pl.BlockSpec((tm, LANES), lambda b, j: (j, 0)),
                const((CONV_WIDTH, C_CONV)), const((1, C_CONV)), const((1, C_CONV)), const((1, C_CONV))]
    return pl.pallas_call(
        functools.partial(_front_prompt_kernel, tm=tm, tk=tk, conv_rows=64),
        grid=(B, S // tm), in_specs=in_specs, out_specs=out_specs, out_shape=out_shape,
        scratch_shapes=[pltpu.VMEM((tm + 32, C_CONV), F32),
                        pltpu.VMEM((SUBLANES - 1, tm + 24, C_CONV), F32)],
        compiler_params=_cparams(2), name="front_prompt",
    )(x, g_pre, w_in_b, cos, sin_signed, w_dw, b_dw, ln_g, ln_b)


def _attn_prompt_kernel(qT_ref, qi2_ref, wT_ref, k_ref, vT_ref, ki_ref, kn2_ref, out_ref,
                        sc_ref, qz_ref, acc_ref, m_ref, *, tq, tk, topk, idx_bits):
    i = pl.program_id(1)
    q0 = i * tq
    n_diag = tq // tk
    n_full = i * n_diag
    n_chunks = n_full + n_diag
    n_rg = tk // SUBLANES
    row_iota = lax.broadcasted_iota(I32, (tk, tq), 0)
    qpos = q0 + lax.broadcasted_iota(I32, (tk, tq), 1)

    def score_chunk(c, causal):
        d = jnp.dot(ki_ref[0, c], qi2_ref[0, 0], preferred_element_type=F32)
        s = jnp.zeros((tk, tq), F32)
        for h in range(H_IDX):
            s = s + wT_ref[0, h:h + 1, :] * jnp.maximum(d[:, h * tq:(h + 1) * tq], 0.0)
        if causal:
            s = jnp.where(c * tk + row_iota <= qpos, s, -jnp.inf)
        sc_ref[c] = s

    def full_body(c, carry):
        score_chunk(c, False)
        return carry
    lax.fori_loop(0, n_full, full_body, 0)
    for dch in range(n_diag):
        score_chunk(n_full + dch, True)

    @pl.when(lax.bitwise_and(n_chunks, 1) == 1)
    def _():
        sc_ref[n_chunks] = jnp.full((tk, tq), -jnp.inf, F32)

    def count(pred):
        def body(g, acc):
            for u in range(2):
                c = 2 * g + u
                m = jnp.where(pred(c, sc_ref[c]), 1.0, 0.0).reshape(n_rg, SUBLANES, tq)
                acc = acc + jnp.sum(m, axis=0)
            return acc
        acc = lax.fori_loop(0, lax.shift_right_logical(n_chunks + 1, 1), body, jnp.zeros((SUBLANES, tq), F32))
        return jnp.sum(acc, axis=0, keepdims=True)

    thr, n_ge = _bisect_threshold(lambda t: count(lambda c, s: s >= t), (1, tq), topk)
    tied = (n_ge > topk) & (thr > -jnp.inf)

    @pl.when(jnp.max(jnp.where(tied, 1.0, 0.0)) > 0.0)
    def _():
        need = topk - count(lambda c, s: s > thr)
        last = _bisect_tie_index(
            lambda cand: count(lambda c, s: (s == thr) & (c * tk + row_iota < cand)),
            need, (1, tq), idx_bits)

        def drop(c, carry):
            s = sc_ref[c]
            sc_ref[c] = jnp.where(tied & (s == thr) & (c * tk + row_iota > last), -jnp.inf, s)
            return carry
        lax.fori_loop(0, n_chunks, drop, 0)

    zeros_half = jnp.zeros((HEAD_DIM, tq), BF16)
    for pr in range(N_HEADS // 2):
        q_even = qT_ref[0, (2 * pr) * HEAD_DIM:(2 * pr + 1) * HEAD_DIM, :]
        q_odd = qT_ref[0, (2 * pr + 1) * HEAD_DIM:(2 * pr + 2) * HEAD_DIM, :]
        qz_ref[pr] = jnp.concatenate([jnp.concatenate([q_even, zeros_half], axis=1),
                                      jnp.concatenate([zeros_half, q_odd], axis=1)], axis=0)
    m_ref[...] = jnp.full((N_HEADS, SUBLANES, tq), NEG_BIG, F32)
    acc_ref[...] = jnp.zeros((N_HEADS * V_ROWS, tq), F32)

    qf = qT_ref[0].astype(F32)
    qn2 = jnp.sum((qf * qf).reshape(N_HEADS, HEAD_DIM, tq), axis=1)
    kn2 = jnp.max(kn2_ref[0], axis=0)[:, 0:1]
    small_logits = jnp.max(qn2 * kn2) * NORM_SLACK <= MAX_RAW_LOGIT ** 2

    def masked_bias(c, causal):
        sel = sc_ref[c] >= thr
        if causal:
            sel = sel & (c * tk + row_iota <= qpos)
        return jnp.where(sel, 0.0, -jnp.inf)

    def attend_chunk_raw(c, causal):
        neg = masked_bias(c, causal)
        probs = []
        for pr in range(N_HEADS // 2):
            d = jnp.dot(k_ref[0, c, :, pr * LANES:(pr + 1) * LANES], qz_ref[pr], preferred_element_type=F32)
            probs += [jnp.exp2(d[:, e * tq:(e + 1) * tq] + neg).astype(BF16) for e in range(2)]
        for h in range(N_HEADS):
            rows = slice(h * V_ROWS, (h + 1) * V_ROWS)
            acc_ref[rows, :] += jnp.dot(vT_ref[0, c, rows, :], probs[h], preferred_element_type=F32)

    def attend_chunk(c, causal):
        neg = masked_bias(c, causal)
        masked, col_max = [], []
        for pr in range(N_HEADS // 2):
            d = jnp.dot(k_ref[0, c, :, pr * LANES:(pr + 1) * LANES], qz_ref[pr], preferred_element_type=F32)
            for e in range(2):
                s = (d[:, e * tq:(e + 1) * tq] + neg).reshape(n_rg, SUBLANES, tq)
                masked.append(s)
                col_max.append(jnp.max(jnp.max(s, axis=0), axis=0, keepdims=True))
        for h in range(N_HEADS):
            s = masked[h]
            m_old = m_ref[h]
            m_new = jnp.maximum(m_old, col_max[h])
            alpha = jnp.exp2(m_old - m_new)
            p = jnp.exp2(s - m_new[None]).reshape(tk, tq).astype(BF16)
            m_ref[h] = m_new
            rows = slice(h * V_ROWS, (h + 1) * V_ROWS)
            pv = jnp.dot(vT_ref[0, c, rows, :], p, preferred_element_type=F32)
            acc = acc_ref[rows, :].reshape(V_ROWS // SUBLANES, SUBLANES, tq) * alpha[None]
            acc_ref[rows, :] = acc.reshape(V_ROWS, tq) + pv

    def attend_all(chunk_fn):
        def full(c, carry):
            chunk_fn(c, False)
            return carry
        lax.fori_loop(0, n_full, full, 0)
        for dch in range(n_diag):
            chunk_fn(n_full + dch, True)

    @pl.when(small_logits)
    def _():
        attend_all(attend_chunk_raw)

    @pl.when(jnp.logical_not(small_logits))
    def _():
        attend_all(attend_chunk)

    outs = []
    for h in range(N_HEADS):
        denom = acc_ref[h * V_ROWS + HEAD_DIM:h * V_ROWS + HEAD_DIM + 1, :]
        outs.append(acc_ref[h * V_ROWS:h * V_ROWS + HEAD_DIM, :] / denom)
    out_ref[0] = jnp.concatenate(outs, axis=0).T.astype(BF16)


def _attn_prompt(qT, qi2, wT, k4, vT4, ki4, kn2, *, tq, tk):
    B, n_c = k4.shape[:2]
    S = n_c * tk
    topk = min(TOPK_MAX, S // 4)
    whole = lambda shp: pl.BlockSpec((1,) + shp, lambda b, i: (b,) + (0,) * len(shp))
    colT = lambda r: pl.BlockSpec((1, r, tq), lambda b, i: (b, 0, i))
    return pl.pallas_call(
        functools.partial(_attn_prompt_kernel, tq=tq, tk=tk, topk=topk, idx_bits=int(S).bit_length()),
        grid=(B, S // tq),
        in_specs=[colT(ATTN_DIM), pl.BlockSpec((1, 1, D_IDX, H_IDX * tq), lambda b, i: (b, i, 0, 0)), colT(H_IDX),
                  whole((n_c, tk, ATTN_DIM)), whole((n_c, N_HEADS * V_ROWS, tk)), whole((n_c, tk, D_IDX)),
                  whole(kn2.shape[1:])],
        out_specs=pl.BlockSpec((1, tq, ATTN_DIM), lambda b, i: (b, i, 0)),
        out_shape=jax.ShapeDtypeStruct((B, S, ATTN_DIM), BF16),
        scratch_shapes=[pltpu.VMEM((n_c + 1, tk, tq), F32),
                        pltpu.VMEM((N_HEADS // 2, LANES, 2 * tq), BF16),
                        pltpu.VMEM((N_HEADS * V_ROWS, tq), F32),
                        pltpu.VMEM((N_HEADS, SUBLANES, tq), F32)],
        compiler_params=_cparams(2), name="attn_prompt",
    )(qT, qi2, wT, k4, vT4, ki4, kn2)


def _back_kernel(x_ref, c_ref, a_ref, wo_ref, gpost_ref, gmpre_ref, gmpost_ref, wup_ref, wdn_ref, y_ref,
                 *, ff_chunk):
    mix = (jnp.dot(c_ref[...], wo_ref[0:C_CONV, :], preferred_element_type=F32)
           + jnp.dot(a_ref[...], wo_ref[C_CONV:, :], preferred_element_type=F32))
    x1 = x_ref[...] + _rms(mix) * gpost_ref[...]
    hm = (_rms(x1) * gmpre_ref[...]).astype(BF16)
    m = jnp.zeros_like(x1)
    for f in range(0, D_FF, ff_chunk):
        up = jnp.dot(hm, wup_ref[:, f:f + ff_chunk], preferred_element_type=F32)
        act = jnp.square(jnp.maximum(up, 0.0)).astype(BF16)
        m = m + jnp.dot(act, wdn_ref[f:f + ff_chunk, :], preferred_element_type=F32)
    y_ref[...] = x1 + _rms(m) * gmpost_ref[...]


def _back(x2, conv2, attn2, w_out_b, g_post, g_mpre, g_mpost, w_up_b, w_dn_b, *, tm):
    R = x2.shape[0]
    row = lambda w: pl.BlockSpec((tm, w), lambda r: (r, 0))
    const = lambda shp: pl.BlockSpec(shp, lambda r: (0,) * len(shp))
    return pl.pallas_call(
        functools.partial(_back_kernel, ff_chunk=1024),
        grid=(R // tm,),
        in_specs=[row(D_MODEL), row(C_CONV), row(ATTN_DIM), const((D_MODEL, D_MODEL)),
                  const((1, D_MODEL)), const((1, D_MODEL)), const((1, D_MODEL)),
                  const((D_MODEL, D_FF)), const((D_FF, D_MODEL))],
        out_specs=row(D_MODEL), out_shape=jax.ShapeDtypeStruct((R, D_MODEL), F32),
        compiler_params=_cparams(1), name="back",
    )(x2, conv2, attn2, w_out_b, g_post, g_mpre, g_mpost, w_up_b, w_dn_b)


def _front_sample_kernel(x_ref, g_ref, w_ref, cos_ref, sin_ref, wdw_ref, bdw_ref, lng_ref, lnb_ref, st_ref,
                         k_out, v_out, ki_out, q_out, qi_out, w_out, conv_out, state_out, ext_ref, *, gs, ds):
    a, gate, q, k, v, qi, ki_slab, w_eff = _project(x_ref[...], g_ref[...], w_ref, cos_ref[...], sin_ref[...])
    k_out[...] = k
    v_out[...] = v
    ki_out[...] = ki_slab[:, :D_IDX]
    q_out[...] = q.astype(BF16)
    qi_out[...] = qi.astype(BF16)
    w_out[...] = w_eff

    ext_ref[:, 0:CONV_HALO, :] = st_ref[...]
    ext_ref[:, CONV_HALO:CONV_HALO + ds, :] = _glu(a, gate).reshape(gs, ds, C_CONV)
    acc = jnp.zeros((gs, ds, C_CONV), F32)
    for t in range(CONV_WIDTH):
        acc = acc + wdw_ref[t:t + 1, :] * ext_ref[:, t:t + ds, :]
    y = _ln_swish(acc.reshape(gs * ds, C_CONV) + bdw_ref[...], lng_ref[...], lnb_ref[...])
    conv_out[...] = y.astype(BF16)
    state_out[...] = ext_ref[:, ds:ds + CONV_HALO, :]


def _front_sample(x2, g_pre, w_in_b, cos, sin_signed, w_dw, b_dw, ln_g, ln_b, state, *, gs, ds):
    R = x2.shape[0]
    DB = R // ds
    tm = gs * ds
    row = lambda w: pl.BlockSpec((tm, w), lambda r: (r, 0))
    const = lambda shp: pl.BlockSpec(shp, lambda r: (0,) * len(shp))
    st_spec = pl.BlockSpec((gs, CONV_HALO, C_CONV), lambda r: (r, 0, 0))
    out_shape = (
        jax.ShapeDtypeStruct((R, ATTN_DIM), F32), jax.ShapeDtypeStruct((R, ATTN_DIM), F32),
        jax.ShapeDtypeStruct((R, D_IDX), F32),
        jax.ShapeDtypeStruct((R, ATTN_DIM), BF16), jax.ShapeDtypeStruct((R, H_IDX * D_IDX), BF16),
        jax.ShapeDtypeStruct((R, LANES), F32),
        jax.ShapeDtypeStruct((R, C_CONV), BF16), jax.ShapeDtypeStruct((DB, CONV_HALO, C_CONV), F32))
    out_specs = (row(ATTN_DIM), row(ATTN_DIM), row(D_IDX), row(ATTN_DIM), row(H_IDX * D_IDX), row(LANES),
                 row(C_CONV), st_spec)
    return pl.pallas_call(
        functools.partial(_front_sample_kernel, gs=gs, ds=ds),
        grid=(R // tm,),
        in_specs=[row(D_MODEL), const((1, D_MODEL)), const((D_MODEL, D_IN_PAD)),
                  const((tm, LANES)), const((tm, LANES)),
                  const((CONV_WIDTH, C_CONV)), const((1, C_CONV)), const((1, C_CONV)), const((1, C_CONV)),
                  st_spec],
        out_specs=out_specs, out_shape=out_shape,
        scratch_shapes=[pltpu.VMEM((gs, CONV_HALO + ds + 2, C_CONV), F32)],
        compiler_params=_cparams(1), name="front_sample",
    )(x2, g_pre, w_in_b, cos, sin_signed, w_dw, b_dw, ln_g, ln_b, state)


def _idx_sample_kernel(pt_ref, *refs, pg, n_pages, ds, gb, topk, idx_bits):
    page_refs = refs[:pg]
    qi_ref, w_ref, kin_ref, sc_ref, thr_ref, wb_ref = refs[pg:]
    s_id = pl.program_id(1)
    bb = pl.program_id(0) % gb
    n_steps = n_pages // pg
    lane = lax.broadcasted_iota(I32, (ds, PAGE_SIZE), 1)
    trow = lax.broadcasted_iota(I32, (ds, PAGE_SIZE), 0)

    @pl.when(s_id == 0)
    def _():
        for h in range(H_IDX):
            wb_ref[h] = jnp.broadcast_to(w_ref[0, :, h:h + 1], (ds, PAGE_SIZE))

    def scores(keys_b, keys_on_lanes):
        contract = (((1,), (0,)), ((), ())) if keys_on_lanes else (((1,), (1,)), ((), ()))
        d = lax.dot_general(qi_ref[0], keys_b, contract, preferred_element_type=F32)
        s = jnp.zeros((ds, PAGE_SIZE), F32)
        for h in range(H_IDX):
            s = s + wb_ref[h] * jnp.maximum(d[h * ds:(h + 1) * ds], 0.0)
        return s

    for p in range(pg):
        sc_ref[bb, s_id * pg + p] = scores(page_refs[p][0].astype(BF16), True)

    @pl.when(s_id == n_steps - 1)
    def _():
        kin = jnp.concatenate([kin_ref[0], jnp.zeros((PAGE_SIZE - ds, D_IDX), BF16)], axis=0)
        sc_ref[bb, n_pages] = jnp.where(lane <= trow, scores(kin, False), -jnp.inf)

    @pl.when((s_id == n_steps - 1) & (bb == gb - 1))
    def _():
        shape4 = (gb, n_pages + 1, ds, PAGE_SIZE)

        def count(pred):
            hits = jnp.where(pred(sc_ref[...]), 1.0, 0.0)
            return jnp.sum(jnp.sum(hits, axis=1, keepdims=True), axis=3, keepdims=True)

        thr, n_ge = _bisect_threshold(lambda t: count(lambda s: s >= t), (gb, 1, ds, 1), topk)
        tied = (n_ge > topk) & (thr > -jnp.inf)

        @pl.when(jnp.max(jnp.where(tied, 1.0, 0.0)) > 0.0)
        def _():
            key_idx = (lax.broadcasted_iota(I32, shape4, 1) * PAGE_SIZE
                       + lax.broadcasted_iota(I32, shape4, 3))
            need = topk - count(lambda s: s > thr)
            last = _bisect_tie_index(lambda cand: count(lambda s: (s == thr) & (key_idx < cand)),
                                     need, (gb, 1, ds, 1), idx_bits)
            s = sc_ref[...]
            sc_ref[...] = jnp.where(tied & (s == thr) & (key_idx > last), -jnp.inf, s)

        thr_ref[...] = jnp.broadcast_to(thr.reshape(gb, ds, 1), (gb, ds, PAGE_SIZE))


def _idx_sample(page_table_flat, kidx_pages, qi_st, w_s, ki_new_b, *, pg, n_pages, ds):
    DB = qi_st.shape[0]
    gb = _pick(DB, (16, 8, 4, 2, 1))
    L = n_pages * PAGE_SIZE + ds
    topk = min(TOPK_MAX, L // 4)
    page_spec = lambda p: pl.BlockSpec((1, D_IDX, PAGE_SIZE),
                                       lambda b, s, pt: (pt[b * n_pages + s * pg + p], 0, 0))
    per_seq = lambda shp: pl.BlockSpec((1,) + shp, lambda b, s, pt: (b,) + (0,) * len(shp))
    per_group = lambda shp: pl.BlockSpec((gb,) + shp, lambda b, s, pt: (b // gb,) + (0,) * len(shp))
    grid_spec = pltpu.PrefetchScalarGridSpec(
        num_scalar_prefetch=1, grid=(DB, n_pages // pg),
        in_specs=[page_spec(p) for p in range(pg)]
        + [per_seq((H_IDX * ds, D_IDX)), per_seq((ds, H_IDX)), per_seq((ds, D_IDX))],
        out_specs=[per_group((n_pages + 1, ds, PAGE_SIZE)), per_group((ds, PAGE_SIZE))],
        scratch_shapes=[pltpu.VMEM((H_IDX, ds, PAGE_SIZE), F32)])
    return pl.pallas_call(
        functools.partial(_idx_sample_kernel, pg=pg, n_pages=n_pages, ds=ds, gb=gb, topk=topk,
                          idx_bits=int(L).bit_length()),
        grid_spec=grid_spec,
        out_shape=(jax.ShapeDtypeStruct((DB, n_pages + 1, ds, PAGE_SIZE), F32),
                   jax.ShapeDtypeStruct((DB, ds, PAGE_SIZE), F32)),
        compiler_params=_cparams(2), name="idx_sample",
    )(page_table_flat, *([kidx_pages] * pg), qi_st, w_s, ki_new_b)


def _attn_sample_kernel(pt_ref, *refs, pg, n_pages, ds):
    k_refs, v_refs = refs[:pg], refs[pg:2 * pg]
    q_ref, kn_ref, vn_ref, sc_ref, thr_ref, out_ref, qbd_ref, acc_ref, m_ref, l_ref = refs[2 * pg:]
    s_id = pl.program_id(1)
    n_steps = n_pages // pg
    hq = N_HEADS * ds
    head_of_row = lax.broadcasted_iota(I32, (hq, ATTN_DIM), 0) // ds
    head_of_col = lax.broadcasted_iota(I32, (hq, ATTN_DIM), 1) // HEAD_DIM

    @pl.when(s_id == 0)
    def _():
        q_rows = jnp.concatenate([q_ref[0]] * N_HEADS, axis=0)
        qbd_ref[...] = jnp.where(head_of_row == head_of_col, q_rows, jnp.zeros_like(q_rows))
        m_ref[...] = jnp.full((hq, 1), NEG_BIG, F32)
        l_ref[...] = jnp.zeros((hq, 1), F32)
        acc_ref[...] = jnp.zeros((hq, ATTN_DIM), F32)

    nn, nt = (((1,), (0,)), ((), ())), (((1,), (1,)), ((), ()))

    def attend(kbs, vbs, sels, keys_on_lanes):
        ss = [lax.dot_general(qbd_ref[...], kb, nn if keys_on_lanes else nt, preferred_element_type=F32)
              + jnp.concatenate([jnp.where(sel, 0.0, -jnp.inf)] * N_HEADS, axis=0)
              for kb, sel in zip(kbs, sels)]
        blk_max = functools.reduce(jnp.maximum, ss)
        m_old = m_ref[...]
        m_new = jnp.maximum(m_old, jnp.max(blk_max, axis=1, keepdims=True))
        alpha = jnp.exp2(m_old - m_new)
        ps = [jnp.exp2(s - m_new) for s in ss]
        l_ref[...] = alpha * l_ref[...] + jnp.sum(functools.reduce(jnp.add, ps), axis=1, keepdims=True)
        m_ref[...] = m_new
        pv = functools.reduce(jnp.add, [
            lax.dot_general(p.astype(BF16), vb, nt if keys_on_lanes else nn, preferred_element_type=F32)
            for p, vb in zip(ps, vbs)])
        acc_ref[...] = alpha * acc_ref[...] + pv

    thr = thr_ref[0]
    attend([r[0].astype(BF16) for r in k_refs], [r[0].astype(BF16) for r in v_refs],
           [sc_ref[0, s_id * pg + p] >= thr for p in range(pg)], True)

    @pl.when(s_id == n_steps - 1)
    def _():
        pad = jnp.zeros((PAGE_SIZE - ds, ATTN_DIM), BF16)
        lane = lax.broadcasted_iota(I32, (ds, PAGE_SIZE), 1)
        trow = lax.broadcasted_iota(I32, (ds, PAGE_SIZE), 0)
        sel = (sc_ref[0, n_pages] >= thr) & (lane <= trow)
        attend([jnp.concatenate([kn_ref[0], pad], axis=0)], [jnp.concatenate([vn_ref[0], pad], axis=0)],
               [sel], False)
        o = jnp.where(head_of_row == head_of_col, acc_ref[...] / l_ref[...], 0.0)
        out = o[0:ds]
        for h in range(1, N_HEADS):
            out = out + o[h * ds:(h + 1) * ds]
        out_ref[0] = out.astype(BF16)


def _attn_sample(page_table_flat, k_pages, v_pages, q_s, k_new_b, v_new_b, scores, thr, *, pg, n_pages, ds):
    DB = q_s.shape[0]
    page_spec = lambda p: pl.BlockSpec((1, ATTN_DIM, PAGE_SIZE),
                                       lambda b, s, pt: (pt[b * n_pages + s * pg + p], 0, 0))
    per_seq = lambda shp: pl.BlockSpec((1,) + shp, lambda b, s, pt: (b,) + (0,) * len(shp))
    hq = N_HEADS * ds
    grid_spec = pltpu.PrefetchScalarGridSpec(
        num_scalar_prefetch=1, grid=(DB, n_pages // pg),
        in_specs=[page_spec(p) for p in range(pg)] * 2
        + [per_seq((ds, ATTN_DIM))] * 3 + [per_seq((n_pages + 1, ds, PAGE_SIZE)), per_seq((ds, PAGE_SIZE))],
        out_specs=per_seq((ds, ATTN_DIM)),
        scratch_shapes=[pltpu.VMEM((hq, ATTN_DIM), BF16), pltpu.VMEM((hq, ATTN_DIM), F32),
                        pltpu.VMEM((hq, 1), F32), pltpu.VMEM((hq, 1), F32)])
    return pl.pallas_call(
        functools.partial(_attn_sample_kernel, pg=pg, n_pages=n_pages, ds=ds),
        grid_spec=grid_spec, out_shape=jax.ShapeDtypeStruct((DB, ds, ATTN_DIM), BF16),
        compiler_params=_cparams(2), name="attn_sample",
    )(page_table_flat, *([k_pages] * pg), *([v_pages] * pg), q_s, k_new_b, v_new_b, scores, thr)


def _rope_tables(pos):
    half = HEAD_DIM // 2
    inv = ROPE_THETA ** (-jnp.arange(half, dtype=F32) / half)
    ang = pos.astype(F32)[:, None] * inv[None, :]
    cos, sin = jnp.cos(ang), jnp.sin(ang)
    return jnp.tile(cos, (1, 4)), jnp.tile(jnp.concatenate([-sin, sin], axis=1), (1, 2))


def _pick(n, prefs):
    for p in prefs:
        if n % p == 0:
            return p
    raise ValueError(f"no supported tile for extent {n}")


def kernel(x_prompt, x_sample, cache_k, cache_v, cache_kidx, state_conv, page_table, norm_attn_pre,
           norm_attn_post, w_in, w_dw, b_dw, conv_ln_g, conv_ln_b, w_out, norm_mlp_pre, norm_mlp_post,
           w_up, w_down):
    B, S, _ = x_prompt.shape
    DB, DS, _ = x_sample.shape
    n_pages = page_table.shape[1]
    past_len = n_pages * PAGE_SIZE
    assert w_in.shape[0] == 1, "single-layer kernel"
    assert DS == SUBLANES, "sample rows per sequence must fill one sublane tile"

    w_in_b = jnp.pad(w_in[0], ((0, 0), (0, D_IN_PAD - D_IN))).astype(BF16)
    w_out_b, w_up_b, w_dn_b = w_out[0].astype(BF16), w_up[0].astype(BF16), w_down[0].astype(BF16)
    g_pre, g_post = norm_attn_pre, norm_attn_post
    g_mpre, g_mpost = norm_mlp_pre, norm_mlp_post

    tm = _pick(S, (256, 128))
    tk = _pick(S, (256, 128))
    tq = _pick(S, (256, 128))
    cos_p, sin_p = _rope_tables(jnp.arange(S, dtype=I32))
    assert tm == tq, "the front kernel lays qi out per attention query tile"
    (kT_p, vT_p, kiT_p, k4, vT4, qT, qi2, wT, ki4, kn2, conv_p, state_p) = _front_prompt(
        x_prompt, g_pre, w_in_b, cos_p, sin_p, w_dw[0], b_dw, conv_ln_g, conv_ln_b, tm=tm, tk=tk)
    attn_p = _attn_prompt(qT, qi2, wT, k4, vT4, ki4, kn2, tq=tq, tk=tk)
    y_p = _back(x_prompt.reshape(B * S, D_MODEL), conv_p.reshape(B * S, C_CONV),
                attn_p.reshape(B * S, ATTN_DIM), w_out_b, g_post, g_mpre, g_mpost, w_up_b, w_dn_b,
                tm=tm).reshape(B, S, D_MODEL)

    R = DB * DS
    gs = _pick(DB, (32, 16, 8, 4, 2, 1))
    cos_s, sin_s = _rope_tables(past_len + jnp.arange(DS, dtype=I32))
    cos_s, sin_s = jnp.tile(cos_s, (gs, 1)), jnp.tile(sin_s, (gs, 1))
    (k_s, v_s, ki_s, q_s, qi_s, w_s, conv_s, state_s) = _front_sample(
        x_sample.reshape(R, D_MODEL), g_pre, w_in_b, cos_s, sin_s, w_dw[0], b_dw, conv_ln_g, conv_ln_b,
        state_conv[0], gs=gs, ds=DS)
    pg = _pick(n_pages, (16, 8, 4, 2, 1))
    pt_flat = page_table.reshape(-1)
    qi_st = qi_s.reshape(DB, DS, H_IDX, D_IDX).transpose(0, 2, 1, 3).reshape(DB, H_IDX * DS, D_IDX)
    w_sq = w_s[:, D_IDX:D_IDX + H_IDX].reshape(DB, DS, H_IDX)
    n_phys = cache_k.shape[1]
    pagesT = lambda c: jnp.transpose(c[0], (0, 2, 3, 1)).reshape(n_phys, ATTN_DIM, PAGE_SIZE)
    scores, thr = _idx_sample(pt_flat, jnp.transpose(cache_kidx[0], (0, 2, 1)), qi_st, w_sq,
                              ki_s.astype(BF16).reshape(DB, DS, D_IDX), pg=pg, n_pages=n_pages, ds=DS)
    attn_s = _attn_sample(pt_flat, pagesT(cache_k), pagesT(cache_v),
                          q_s.reshape(DB, DS, ATTN_DIM), k_s.astype(BF16).reshape(DB, DS, ATTN_DIM),
                          v_s.astype(BF16).reshape(DB, DS, ATTN_DIM), scores, thr,
                          pg=pg, n_pages=n_pages, ds=DS)
    y_s = _back(x_sample.reshape(R, D_MODEL), conv_s, attn_s.reshape(R, ATTN_DIM), w_out_b, g_post, g_mpre,
                g_mpost, w_up_b, w_dn_b, tm=_pick(R, (256, 128, 64, 32, 16, 8))).reshape(DB, DS, D_MODEL)

    hd = (N_HEADS, HEAD_DIM)
    unT = lambda t: jnp.transpose(t.reshape(B, N_HEADS, HEAD_DIM, S), (0, 3, 1, 2))[None]
    return (y_p, y_s,
            unT(kT_p), unT(vT_p), jnp.transpose(kiT_p, (0, 2, 1))[None], state_p[None],
            k_s.reshape(1, DB, DS, *hd), v_s.reshape(1, DB, DS, *hd), ki_s.reshape(1, DB, DS, D_IDX),
            state_s[None])
```

```python
import functools

import jax
import jax.numpy as jnp
import numpy as np
from jax import lax
from jax.experimental import pallas as pl
from jax.experimental.pallas import tpu as pltpu

F32 = jnp.float32
BF16 = jnp.bfloat16
I32 = jnp.int32

D_MODEL = 1024
C_CONV = 512
CONV_WIDTH = 31
CONV_HALO = CONV_WIDTH - 1
N_HEADS = 8
HEAD_DIM = 64
ATTN_DIM = N_HEADS * HEAD_DIM
H_IDX = 8
D_IDX = 64
TOPK_MAX = 256
D_FF = 4 * D_MODEL
ROPE_THETA = 10000.0
EPS = 1e-6
PAGE_SIZE = 128

COL_A, COL_G, COL_Q, COL_K, COL_V, COL_QI, COL_KW = 0, 512, 1024, 1536, 2048, 2560, 3072
D_IN = 3144
LANES = 128
SUBLANES = 8
D_IN_PAD = COL_KW + LANES

INT_MIN = -(2 ** 31)
F32_INF_BITS = 0x7F800000
NEG_BIG = -1e30
BF16_ROWS = 16
V_ROWS = HEAD_DIM + BF16_ROWS
LOG2_E = 1.4426950408889634
MAX_RAW_LOGIT = 64.0
NORM_SLACK = 1.05

VMEM_LIMIT = 56 * 1024 * 1024


def _cparams(n_axes):
    return pltpu.CompilerParams(dimension_semantics=("arbitrary",) * n_axes,
                                vmem_limit_bytes=VMEM_LIMIT)


def _rms(x):
    return x * lax.rsqrt(jnp.mean(x * x, axis=-1, keepdims=True) + EPS)


def _rope_slab(x, cos, sin_signed, first_half):
    outs = []
    for c in range(x.shape[1] // LANES):
        xc = x[:, c * LANES:(c + 1) * LANES]
        partner = jnp.where(first_half, pltpu.roll(xc, LANES - 32, 1), pltpu.roll(xc, 32, 1))
        outs.append(xc * cos + partner * sin_signed)
    return outs[0] if len(outs) == 1 else jnp.concatenate(outs, axis=1)


def _project(x, g, w_ref, cos, sin_signed):
    hb = (_rms(x) * g).astype(BF16)
    dot = lambda c0, n: jnp.dot(hb, w_ref[:, c0:c0 + n], preferred_element_type=F32)
    first_half = (lax.broadcasted_iota(I32, (x.shape[0], LANES), 1) % 64) < 32
    a = dot(COL_A, C_CONV)
    gate = dot(COL_G, C_CONV)
    q = _rope_slab(dot(COL_Q, ATTN_DIM), cos, sin_signed, first_half) * (HEAD_DIM ** -0.5 * LOG2_E)
    k = _rope_slab(dot(COL_K, ATTN_DIM), cos, sin_signed, first_half)
    v = dot(COL_V, ATTN_DIM)
    qi = _rope_slab(dot(COL_QI, H_IDX * D_IDX), cos, sin_signed, first_half)
    kw = dot(COL_KW, LANES)
    ki_slab = _rope_slab(kw, cos, sin_signed, first_half)
    w_eff = kw * ((H_IDX ** -0.5) * (D_IDX ** -0.5))
    return a, gate, q, k, v, qi, ki_slab, w_eff


def _glu(a, gate):
    return a * (1.0 / (1.0 + jnp.exp(-gate)))


def _ln_swish(y, ln_g, ln_b):
    mu = jnp.mean(y, axis=-1, keepdims=True)
    d = y - mu
    var = jnp.mean(d * d, axis=-1, keepdims=True)
    y = (d * lax.rsqrt(var + EPS)) * ln_g + ln_b
    return y * (1.0 / (1.0 + jnp.exp(-y)))


def _key_to_f32(c):
    c = jnp.maximum(c, -F32_INF_BITS)
    bits = jnp.where(c >= 0, c, INT_MIN - c)
    return lax.bitcast_convert_type(bits, F32)


def _f32_to_key(x):
    bits = lax.bitcast_convert_type(x, I32)
    return jnp.where(bits >= 0, bits, INT_MIN - bits)


def _bisect_threshold(count_ge, lo, hi, topk):
    def still_open(lo, hi, cnt):
        return (cnt != topk) & (hi > lo + 1)

    def step(lo, hi, cnt):
        act = still_open(lo, hi, cnt)
        mid = jnp.right_shift(lo, 1) + jnp.right_shift(hi, 1) + (lo & hi & 1)
        c = count_ge(_key_to_f32(mid))
        take = act & (c >= topk)
        drop = act & (c < topk)
        return jnp.where(take, mid, lo), jnp.where(drop, mid, hi), jnp.where(take, c, cnt)

    def body(st):
        lo, hi, cnt, _ = st
        lo, hi, cnt = step(*step(lo, hi, cnt))
        return lo, hi, cnt, jnp.max(jnp.where(still_open(lo, hi, cnt), 1.0, 0.0)) > 0.0

    init = (lo, hi, jnp.full(lo.shape, jnp.inf, F32), jnp.bool_(True))
    lo, _, cnt, _ = lax.while_loop(lambda st: st[3], body, init)
    return _key_to_f32(lo), cnt


def _bisect_tie_index(count_eq_below, need, shape, nbits):
    def body(b, c):
        cand = c + lax.shift_left(jnp.int32(1), jnp.int32(nbits - 1) - b)
        return jnp.where(count_eq_below(cand) < need, cand, c)
    return lax.fori_loop(0, nbits, body, jnp.zeros(shape, I32))


def _front_prompt_kernel(x_ref, g_ref, w_ref, cos_ref, sin_ref, wdw_ref, bdw_ref, lng_ref, lnb_ref,
                         kT_out, vT_out, kiT_out, kb_out, vTb_out, qT_out, qi2_out, wT_out, kib_out, kn2_out,
                         conv_out, state_out, ext_ref, sh_ref, *, tm, tk, conv_rows):
    j = pl.program_id(1)
    a, gate, q, k, v, qi, ki_slab, w_eff = _project(x_ref[0], g_ref[...], w_ref, cos_ref[...], sin_ref[...])

    kT = k.T
    kT_out[0] = kT
    kn2 = jnp.max(jnp.sum((kT * kT).reshape(N_HEADS, HEAD_DIM, tm), axis=1), axis=1, keepdims=True)
    kn2_out[0, 0] = jnp.broadcast_to(kn2, (N_HEADS, LANES))
    vT = v.T
    vT_out[0] = vT
    kiT_out[0] = ki_slab.T[:D_IDX]
    kb = k.astype(BF16)
    vTb = vT.astype(BF16)
    ones = jnp.ones((V_ROWS - HEAD_DIM, tm), BF16)
    vTx = jnp.concatenate([blk for h in range(N_HEADS)
                           for blk in (vTb[h * HEAD_DIM:(h + 1) * HEAD_DIM], ones)], axis=0)
    kib = ki_slab[:, :D_IDX].astype(BF16)
    for c in range(tm // tk):
        kb_out[0, c] = kb[c * tk:(c + 1) * tk]
        vTb_out[0, c] = vTx[:, c * tk:(c + 1) * tk]
        kib_out[0, c] = kib[c * tk:(c + 1) * tk]
    qT_out[0] = q.T.astype(BF16)
    qiT = qi.T.astype(BF16)
    qi2_out[0, 0] = jnp.concatenate([qiT[h * D_IDX:(h + 1) * D_IDX] for h in range(H_IDX)], axis=1)
    wT_out[0] = w_eff.T[D_IDX:D_IDX + H_IDX]

    @pl.when(j == 0)
    def _():
        ext_ref[0:32, :] = jnp.zeros((32, C_CONV), F32)
    ext_ref[32:32 + tm, :] = _glu(a, gate)
    for r in range(1, SUBLANES):
        sh_ref[r - 1] = ext_ref[r:r + tm + 24, :]
    for r0 in range(0, tm, conv_rows):
        acc = jnp.zeros((conv_rows, C_CONV), F32)
        for t in range(CONV_WIDTH):
            lo = r0 + 32 - CONV_HALO + t
            r = lo % SUBLANES
            src = ext_ref if r == 0 else sh_ref.at[r - 1]
            acc = acc + wdw_ref[t:t + 1, :] * src[lo - r:lo - r + conv_rows, :]
        y = _ln_swish(acc + bdw_ref[...], lng_ref[...], lnb_ref[...])
        conv_out[0, r0:r0 + conv_rows, :] = y.astype(BF16)
    state_out[0] = ext_ref[32 + tm - CONV_HALO:32 + tm, :]
    ext_ref[0:32, :] = ext_ref[tm:tm + 32, :]


def _front_prompt(x, g_pre, w_in_b, cos, sin_signed, w_dw, b_dw, ln_g, ln_b, *, tm, tk):
    B, S, _ = x.shape
    n_c = S // tk
    row = lambda w: pl.BlockSpec((1, tm, w), lambda b, j: (b, j, 0))
    const = lambda shp: pl.BlockSpec(shp, lambda b, j: (0,) * len(shp))
    chunked = lambda d2, d3: pl.BlockSpec((1, tm // tk, d2, d3), lambda b, j: (b, j, 0, 0))
    colT = lambda r: pl.BlockSpec((1, r, tm), lambda b, j: (b, 0, j))
    out_shape = (
        jax.ShapeDtypeStruct((B, ATTN_DIM, S), F32),
        jax.ShapeDtypeStruct((B, ATTN_DIM, S), F32),
        jax.ShapeDtypeStruct((B, D_IDX, S), F32),
        jax.ShapeDtypeStruct((B, n_c, tk, ATTN_DIM), BF16),
        jax.ShapeDtypeStruct((B, n_c, N_HEADS * V_ROWS, tk), BF16),
        jax.ShapeDtypeStruct((B, ATTN_DIM, S), BF16),
        jax.ShapeDtypeStruct((B, S // tm, D_IDX, H_IDX * tm), BF16),
        jax.ShapeDtypeStruct((B, H_IDX, S), F32),
        jax.ShapeDtypeStruct((B, n_c, tk, D_IDX), BF16),
        jax.ShapeDtypeStruct((B, S // tm, N_HEADS, LANES), F32),
        jax.ShapeDtypeStruct((B, S, C_CONV), BF16),
        jax.ShapeDtypeStruct((B, CONV_HALO, C_CONV), F32),
    )
    out_specs = (colT(ATTN_DIM), colT(ATTN_DIM), colT(D_IDX), chunked(tk, ATTN_DIM),
                 chunked(N_HEADS * V_ROWS, tk), colT(ATTN_DIM),
                 pl.BlockSpec((1, 1, D_IDX, H_IDX * tm), lambda b, j: (b, j, 0, 0)),
                 colT(H_IDX), chunked(tk, D_IDX),
                 pl.BlockSpec((1, 1, N_HEADS, LANES), lambda b, j: (b, j, 0, 0)), row(C_CONV),
                 pl.BlockSpec((1, CONV_HALO, C_CONV), lambda b, j: (b, 0, 0)))
    in_specs = [row(D_MODEL), const((1, D_MODEL)), const((D_MODEL, D_IN_PAD)),
                pl.BlockSpec((tm, LANES), lambda b, j: (j, 0)), pl.BlockSpec((tm, LANES), lambda b, j: (j, 0)),
                const((CONV_WIDTH, C_CONV)), const((1, C_CONV)), const((1, C_CONV)), const((1, C_CONV))]
    return pl.pallas_call(
        functools.partial(_front_prompt_kernel, tm=tm, tk=tk, conv_rows=64),
        grid=(B, S // tm), in_specs=in_specs, out_specs=out_specs, out_shape=out_shape,
        scratch_shapes=[pltpu.VMEM((tm + 32, C_CONV), F32),
                        pltpu.VMEM((SUBLANES - 1, tm + 24, C_CONV), F32)],
        compiler_params=_cparams(2), name="front_prompt",
    )(x, g_pre, w_in_b, cos, sin_signed, w_dw, b_dw, ln_g, ln_b)


def _attn_prompt_kernel(qT_ref, qi2_ref, wT_ref, k_ref, vT_ref, ki_ref, kn2_ref, out_ref,
                        sc_ref, gm_ref, qz_ref, acc_ref, m_ref, *, tq, tk, topk, idx_bits):
    i = pl.program_id(1)
    q0 = i * tq
    n_diag = tq // tk
    n_full = i * n_diag
    n_chunks = n_full + n_diag
    n_rg = tk // SUBLANES
    row_iota = lax.broadcasted_iota(I32, (tk, tq), 0)
    qpos = q0 + lax.broadcasted_iota(I32, (tk, tq), 1)

    def score_chunk(c, causal):
        d = jnp.dot(ki_ref[0, c], qi2_ref[0, 0], preferred_element_type=F32)
        s = jnp.zeros((tk, tq), F32)
        for h in range(H_IDX):
            s = s + wT_ref[0, h:h + 1, :] * jnp.maximum(d[:, h * tq:(h + 1) * tq], 0.0)
        if causal:
            s = jnp.where(c * tk + row_iota <= qpos, s, -jnp.inf)
        sc_ref[c] = s
        gm_ref[...] = jnp.maximum(gm_ref[...], s)

    gm_ref[...] = jnp.full((tk, tq), -jnp.inf, F32)

    def full_body(c, carry):
        score_chunk(c, False)
        return carry
    lax.fori_loop(0, n_full, full_body, 0)
    for dch in range(n_diag):
        score_chunk(n_full + dch, True)

    @pl.when(lax.bitwise_and(n_chunks, 1) == 1)
    def _():
        sc_ref[n_chunks] = jnp.full((tk, tq), -jnp.inf, F32)

    def count(pred):
        def body(g, acc):
            for u in range(2):
                c = 2 * g + u
                m = jnp.where(pred(c, sc_ref[c]), 1.0, 0.0).reshape(n_rg, SUBLANES, tq)
                acc = acc + jnp.sum(m, axis=0)
            return acc
        acc = lax.fori_loop(0, lax.shift_right_logical(n_chunks + 1, 1), body, jnp.zeros((SUBLANES, tq), F32))
        return jnp.sum(acc, axis=0, keepdims=True)

    gm = gm_ref[...]
    hi_key = _f32_to_key(jnp.max(gm, axis=0, keepdims=True)) + 1
    lo_val = jnp.min(gm, axis=0, keepdims=True) if tk >= topk else jnp.full((1, tq), -jnp.inf, F32)
    thr, n_ge = _bisect_threshold(lambda t: count(lambda c, s: s >= t), _f32_to_key(lo_val), hi_key, topk)
    unknown = (n_ge == jnp.inf) & (thr > -jnp.inf)
    n_ge = lax.cond(jnp.max(jnp.where(unknown, 1.0, 0.0)) > 0.0,
                    lambda: jnp.where(unknown, count(lambda c, s: s >= thr), n_ge), lambda: n_ge)
    tied = (n_ge > topk) & (thr > -jnp.inf)

    @pl.when(jnp.max(jnp.where(tied, 1.0, 0.0)) > 0.0)
    def _():
        need = topk - count(lambda c, s: s > thr)
        last = _bisect_tie_index(
            lambda cand: count(lambda c, s: (s == thr) & (c * tk + row_iota < cand)),
            need, (1, tq), idx_bits)

        def drop(c, carry):
            s = sc_ref[c]
            sc_ref[c] = jnp.where(tied & (s == thr) & (c * tk + row_iota > last), -jnp.inf, s)
            return carry
        lax.fori_loop(0, n_chunks, drop, 0)

    zeros_half = jnp.zeros((HEAD_DIM, tq), BF16)
    for pr in range(N_HEADS // 2):
        q_even = qT_ref[0, (2 * pr) * HEAD_DIM:(2 * pr + 1) * HEAD_DIM, :]
        q_odd = qT_ref[0, (2 * pr + 1) * HEAD_DIM:(2 * pr + 2) * HEAD_DIM, :]
        qz_ref[pr] = jnp.concatenate([jnp.concatenate([q_even, zeros_half], axis=1),
                                      jnp.concatenate([zeros_half, q_odd], axis=1)], axis=0)
    m_ref[...] = jnp.full((N_HEADS, SUBLANES, tq), NEG_BIG, F32)
    acc_ref[...] = jnp.zeros((N_HEADS * V_ROWS, tq), F32)

    qf = qT_ref[0].astype(F32)
    qn2 = jnp.sum((qf * qf).reshape(N_HEADS, HEAD_DIM, tq), axis=1)
    kn2 = jnp.max(kn2_ref[0], axis=0)[:, 0:1]
    small_logits = jnp.max(qn2 * kn2) * NORM_SLACK <= MAX_RAW_LOGIT ** 2

    def masked_bias(c, causal):
        sel = sc_ref[c] >= thr
        if causal:
            sel = sel & (c * tk + row_iota <= qpos)
        return jnp.where(sel, 0.0, -jnp.inf)

    def attend_chunk_raw(c, causal):
        neg = masked_bias(c, causal)
        probs = []
        for pr in range(N_HEADS // 2):
            d = jnp.dot(k_ref[0, c, :, pr * LANES:(pr + 1) * LANES], qz_ref[pr], preferred_element_type=F32)
            probs += [jnp.exp2(d[:, e * tq:(e + 1) * tq] + neg).astype(BF16) for e in range(2)]
        for h in range(N_HEADS):
            rows = slice(h * V_ROWS, (h + 1) * V_ROWS)
            acc_ref[rows, :] += jnp.dot(vT_ref[0, c, rows, :], probs[h], preferred_element_type=F32)

    def attend_chunk(c, causal):
        neg = masked_bias(c, causal)
        masked, col_max = [], []
        for pr in range(N_HEADS // 2):
            d = jnp.dot(k_ref[0, c, :, pr * LANES:(pr + 1) * LANES], qz_ref[pr], preferred_element_type=F32)
            for e in range(2):
                s = (d[:, e * tq:(e + 1) * tq] + neg).reshape(n_rg, SUBLANES, tq)
                masked.append(s)
                col_max.append(jnp.max(jnp.max(s, axis=0), axis=0, keepdims=True))
        for h in range(N_HEADS):
            s = masked[h]
            m_old = m_ref[h]
            m_new = jnp.maximum(m_old, col_max[h])
            alpha = jnp.exp2(m_old - m_new)
            p = jnp.exp2(s - m_new[None]).reshape(tk, tq).astype(BF16)
            m_ref[h] = m_new
            rows = slice(h * V_ROWS, (h + 1) * V_ROWS)
            pv = jnp.dot(vT_ref[0, c, rows, :], p, preferred_element_type=F32)
            acc = acc_ref[rows, :].reshape(V_ROWS // SUBLANES, SUBLANES, tq) * alpha[None]
            acc_ref[rows, :] = acc.reshape(V_ROWS, tq) + pv

    def attend_all(chunk_fn):
        def full(c, carry):
            chunk_fn(c, False)
            return carry
        lax.fori_loop(0, n_full, full, 0)
        for dch in range(n_diag):
            chunk_fn(n_full + dch, True)

    @pl.when(small_logits)
    def _():
        attend_all(attend_chunk_raw)

    @pl.when(jnp.logical_not(small_logits))
    def _():
        attend_all(attend_chunk)

    outs = []
    for h in range(N_HEADS):
        denom = acc_ref[h * V_ROWS + HEAD_DIM:h * V_ROWS + HEAD_DIM + 1, :]
        outs.append(acc_ref[h * V_ROWS:h * V_ROWS + HEAD_DIM, :] / denom)
    out_ref[0] = jnp.concatenate(outs, axis=0).T.astype(BF16)


def _attn_prompt(qT, qi2, wT, k4, vT4, ki4, kn2, *, tq, tk):
    B, n_c = k4.shape[:2]
    S = n_c * tk
    topk = min(TOPK_MAX, S // 4)
    whole = lambda shp: pl.BlockSpec((1,) + shp, lambda b, i: (b,) + (0,) * len(shp))
    colT = lambda r: pl.BlockSpec((1, r, tq), lambda b, i: (b, 0, i))
    return pl.pallas_call(
        functools.partial(_attn_prompt_kernel, tq=tq, tk=tk, topk=topk, idx_bits=int(S).bit_length()),
        grid=(B, S // tq),
        in_specs=[colT(ATTN_DIM), pl.BlockSpec((1, 1, D_IDX, H_IDX * tq), lambda b, i: (b, i, 0, 0)), colT(H_IDX),
                  whole((n_c, tk, ATTN_DIM)), whole((n_c, N_HEADS * V_ROWS, tk)), whole((n_c, tk, D_IDX)),
                  whole(kn2.shape[1:])],
        out_specs=pl.BlockSpec((1, tq, ATTN_DIM), lambda b, i: (b, i, 0)),
        out_shape=jax.ShapeDtypeStruct((B, S, ATTN_DIM), BF16),
        scratch_shapes=[pltpu.VMEM((n_c + 1, tk, tq), F32),
                        pltpu.VMEM((tk, tq), F32),
                        pltpu.VMEM((N_HEADS // 2, LANES, 2 * tq), BF16),
                        pltpu.VMEM((N_HEADS * V_ROWS, tq), F32),
                        pltpu.VMEM((N_HEADS, SUBLANES, tq), F32)],
        compiler_params=_cparams(2), name="attn_prompt",
    )(qT, qi2, wT, k4, vT4, ki4, kn2)


def _back_kernel(x_ref, c_ref, a_ref, wo_ref, gpost_ref, gmpre_ref, gmpost_ref, wup_ref, wdn_ref, y_ref,
                 *, ff_chunk):
    mix = (jnp.dot(c_ref[...], wo_ref[0:C_CONV, :], preferred_element_type=F32)
           + jnp.dot(a_ref[...], wo_ref[C_CONV:, :], preferred_element_type=F32))
    x1 = x_ref[...] + _rms(mix) * gpost_ref[...]
    hm = (_rms(x1) * gmpre_ref[...]).astype(BF16)
    m = jnp.zeros_like(x1)
    for f in range(0, D_FF, ff_chunk):
        up = jnp.dot(hm, wup_ref[:, f:f + ff_chunk], preferred_element_type=F32)
        act = jnp.square(jnp.maximum(up, 0.0)).astype(BF16)
        m = m + jnp.dot(act, wdn_ref[f:f + ff_chunk, :], preferred_element_type=F32)
    y_ref[...] = x1 + _rms(m) * gmpost_ref[...]


def _back(x2, conv2, attn2, w_out_b, g_post, g_mpre, g_mpost, w_up_b, w_dn_b, *, tm):
    R = x2.shape[0]
    row = lambda w: pl.BlockSpec((tm, w), lambda r: (r, 0))
    const = lambda shp: pl.BlockSpec(shp, lambda r: (0,) * len(shp))
    return pl.pallas_call(
        functools.partial(_back_kernel, ff_chunk=1024),
        grid=(R // tm,),
        in_specs=[row(D_MODEL), row(C_CONV), row(ATTN_DIM), const((D_MODEL, D_MODEL)),
                  const((1, D_MODEL)), const((1, D_MODEL)), const((1, D_MODEL)),
                  const((D_MODEL, D_FF)), const((D_FF, D_MODEL))],
        out_specs=row(D_MODEL), out_shape=jax.ShapeDtypeStruct((R, D_MODEL), F32),
        compiler_params=_cparams(1), name="back",
    )(x2, conv2, attn2, w_out_b, g_post, g_mpre, g_mpost, w_up_b, w_dn_b)


def _front_sample_kernel(x_ref, g_ref, w_ref, cos_ref, sin_ref, wdw_ref, bdw_ref, lng_ref, lnb_ref, st_ref,
                         k_out, v_out, ki_out, q_out, qi_out, w_out, conv_out, state_out, ext_ref, *, gs, ds):
    a, gate, q, k, v, qi, ki_slab, w_eff = _project(x_ref[...], g_ref[...], w_ref, cos_ref[...], sin_ref[...])
    k_out[...] = k
    v_out[...] = v
    ki_out[...] = ki_slab[:, :D_IDX]
    q_out[...] = q.astype(BF16)
    qi_out[...] = qi.astype(BF16)
    w_out[...] = w_eff

    ext_ref[:, 0:CONV_HALO, :] = st_ref[...]
    ext_ref[:, CONV_HALO:CONV_HALO + ds, :] = _glu(a, gate).reshape(gs, ds, C_CONV)
    acc = jnp.zeros((gs, ds, C_CONV), F32)
    for t in range(CONV_WIDTH):
        acc = acc + wdw_ref[t:t + 1, :] * ext_ref[:, t:t + ds, :]
    y = _ln_swish(acc.reshape(gs * ds, C_CONV) + bdw_ref[...], lng_ref[...], lnb_ref[...])
    conv_out[...] = y.astype(BF16)
    state_out[...] = ext_ref[:, ds:ds + CONV_HALO, :]


def _front_sample(x2, g_pre, w_in_b, cos, sin_signed, w_dw, b_dw, ln_g, ln_b, state, *, gs, ds):
    R = x2.shape[0]
    DB = R // ds
    tm = gs * ds
    row = lambda w: pl.BlockSpec((tm, w), lambda r: (r, 0))
    const = lambda shp: pl.BlockSpec(shp, lambda r: (0,) * len(shp))
    st_spec = pl.BlockSpec((gs, CONV_HALO, C_CONV), lambda r: (r, 0, 0))
    out_shape = (
        jax.ShapeDtypeStruct((R, ATTN_DIM), F32), jax.ShapeDtypeStruct((R, ATTN_DIM), F32),
        jax.ShapeDtypeStruct((R, D_IDX), F32),
        jax.ShapeDtypeStruct((R, ATTN_DIM), BF16), jax.ShapeDtypeStruct((R, H_IDX * D_IDX), BF16),
        jax.ShapeDtypeStruct((R, LANES), F32),
        jax.ShapeDtypeStruct((R, C_CONV), BF16), jax.ShapeDtypeStruct((DB, CONV_HALO, C_CONV), F32))
    out_specs = (row(ATTN_DIM), row(ATTN_DIM), row(D_IDX), row(ATTN_DIM), row(H_IDX * D_IDX), row(LANES),
                 row(C_CONV), st_spec)
    return pl.pallas_call(
        functools.partial(_front_sample_kernel, gs=gs, ds=ds),
        grid=(R // tm,),
        in_specs=[row(D_MODEL), const((1, D_MODEL)), const((D_MODEL, D_IN_PAD)),
                  const((tm, LANES)), const((tm, LANES)),
                  const((CONV_WIDTH, C_CONV)), const((1, C_CONV)), const((1, C_CONV)), const((1, C_CONV)),
                  st_spec],
        out_specs=out_specs, out_shape=out_shape,
        scratch_shapes=[pltpu.VMEM((gs, CONV_HALO + ds + 2, C_CONV), F32)],
        compiler_params=_cparams(1), name="front_sample",
    )(x2, g_pre, w_in_b, cos, sin_signed, w_dw, b_dw, ln_g, ln_b, state)


def _idx_sample_kernel(pt_ref, *refs, pg, n_pages, ds, gb, topk, idx_bits):
    page_refs = refs[:pg]
    qi_ref, w_ref, kin_ref, sc_ref, thr_ref, wb_ref = refs[pg:]
    s_id = pl.program_id(1)
    bb = pl.program_id(0) % gb
    n_steps = n_pages // pg
    lane = lax.broadcasted_iota(I32, (ds, PAGE_SIZE), 1)
    trow = lax.broadcasted_iota(I32, (ds, PAGE_SIZE), 0)

    @pl.when(s_id == 0)
    def _():
        for h in range(H_IDX):
            wb_ref[h] = jnp.broadcast_to(w_ref[0, :, h:h + 1], (ds, PAGE_SIZE))

    def scores(keys_b, keys_on_lanes):
        contract = (((1,), (0,)), ((), ())) if keys_on_lanes else (((1,), (1,)), ((), ()))
        d = lax.dot_general(qi_ref[0], keys_b, contract, preferred_element_type=F32)
        s = jnp.zeros((ds, PAGE_SIZE), F32)
        for h in range(H_IDX):
            s = s + wb_ref[h] * jnp.maximum(d[h * ds:(h + 1) * ds], 0.0)
        return s

    for p in range(pg):
        sc_ref[bb, s_id * pg + p] = scores(page_refs[p][0].astype(BF16), True)

    @pl.when(s_id == n_steps - 1)
    def _():
        kin = jnp.concatenate([kin_ref[0], jnp.zeros((PAGE_SIZE - ds, D_IDX), BF16)], axis=0)
        sc_ref[bb, n_pages] = jnp.where(lane <= trow, scores(kin, False), -jnp.inf)

    @pl.when((s_id == n_steps - 1) & (bb == gb - 1))
    def _():
        shape4 = (gb, n_pages + 1, ds, PAGE_SIZE)

        def count(pred):
            hits = jnp.where(pred(sc_ref[...]), 1.0, 0.0)
            return jnp.sum(jnp.sum(hits, axis=1, keepdims=True), axis=3, keepdims=True)

        shape = (gb, 1, ds, 1)
        thr, n_ge = _bisect_threshold(lambda t: count(lambda s: s >= t), jnp.full(shape, -F32_INF_BITS, I32),
                                      jnp.full(shape, F32_INF_BITS, I32), topk)
        tied = (n_ge > topk) & (thr > -jnp.inf)

        @pl.when(jnp.max(jnp.where(tied, 1.0, 0.0)) > 0.0)
        def _():
            key_idx = (lax.broadcasted_iota(I32, shape4, 1) * PAGE_SIZE
                       + lax.broadcasted_iota(I32, shape4, 3))
            need = topk - count(lambda s: s > thr)
            last = _bisect_tie_index(lambda cand: count(lambda s: (s == thr) & (key_idx < cand)),
                                     need, (gb, 1, ds, 1), idx_bits)
            s = sc_ref[...]
            sc_ref[...] = jnp.where(tied & (s == thr) & (key_idx > last), -jnp.inf, s)

        thr_ref[...] = jnp.broadcast_to(thr.reshape(gb, ds, 1), (gb, ds, PAGE_SIZE))


def _idx_sample(page_table_flat, kidx_pages, qi_st, w_s, ki_new_b, *, pg, n_pages, ds):
    DB = qi_st.shape[0]
    gb = _pick(DB, (16, 8, 4, 2, 1))
    L = n_pages * PAGE_SIZE + ds
    topk = min(TOPK_MAX, L // 4)
    page_spec = lambda p: pl.BlockSpec((1, D_IDX, PAGE_SIZE),
                                       lambda b, s, pt: (pt[b * n_pages + s * pg + p], 0, 0))
    per_seq = lambda shp: pl.BlockSpec((1,) + shp, lambda b, s, pt: (b,) + (0,) * len(shp))
    per_group = lambda shp: pl.BlockSpec((gb,) + shp, lambda b, s, pt: (b // gb,) + (0,) * len(shp))
    grid_spec = pltpu.PrefetchScalarGridSpec(
        num_scalar_prefetch=1, grid=(DB, n_pages // pg),
        in_specs=[page_spec(p) for p in range(pg)]
        + [per_seq((H_IDX * ds, D_IDX)), per_seq((ds, H_IDX)), per_seq((ds, D_IDX))],
        out_specs=[per_group((n_pages + 1, ds, PAGE_SIZE)), per_group((ds, PAGE_SIZE))],
        scratch_shapes=[pltpu.VMEM((H_IDX, ds, PAGE_SIZE), F32)])
    return pl.pallas_call(
        functools.partial(_idx_sample_kernel, pg=pg, n_pages=n_pages, ds=ds, gb=gb, topk=topk,
                          idx_bits=int(L).bit_length()),
        grid_spec=grid_spec,
        out_shape=(jax.ShapeDtypeStruct((DB, n_pages + 1, ds, PAGE_SIZE), F32),
                   jax.ShapeDtypeStruct((DB, ds, PAGE_SIZE), F32)),
        compiler_params=_cparams(2), name="idx_sample",
    )(page_table_flat, *([kidx_pages] * pg), qi_st, w_s, ki_new_b)


def _attn_sample_kernel(pt_ref, *refs, pg, n_pages, ds):
    k_refs, v_refs = refs[:pg], refs[pg:2 * pg]
    q_ref, kn_ref, vn_ref, sc_ref, thr_ref, out_ref, qbd_ref, acc_ref, m_ref, l_ref = refs[2 * pg:]
    s_id = pl.program_id(1)
    n_steps = n_pages // pg
    hq = N_HEADS * ds
    head_of_row = lax.broadcasted_iota(I32, (hq, ATTN_DIM), 0) // ds
    head_of_col = lax.broadcasted_iota(I32, (hq, ATTN_DIM), 1) // HEAD_DIM

    @pl.when(s_id == 0)
    def _():
        q_rows = jnp.concatenate([q_ref[0]] * N_HEADS, axis=0)
        qbd_ref[...] = jnp.where(head_of_row == head_of_col, q_rows, jnp.zeros_like(q_rows))
        m_ref[...] = jnp.full((hq, 1), NEG_BIG, F32)
        l_ref[...] = jnp.zeros((hq, 1), F32)
        acc_ref[...] = jnp.zeros((hq, ATTN_DIM), F32)

    nn, nt = (((1,), (0,)), ((), ())), (((1,), (1,)), ((), ()))

    def attend(kbs, vbs, sels, keys_on_lanes):
        ss = [lax.dot_general(qbd_ref[...], kb, nn if keys_on_lanes else nt, preferred_element_type=F32)
              + jnp.concatenate([jnp.where(sel, 0.0, -jnp.inf)] * N_HEADS, axis=0)
              for kb, sel in zip(kbs, sels)]
        blk_max = functools.reduce(jnp.maximum, ss)
        m_old = m_ref[...]
        m_new = jnp.maximum(m_old, jnp.max(blk_max, axis=1, keepdims=True))
        alpha = jnp.exp2(m_old - m_new)
        ps = [jnp.exp2(s - m_new) for s in ss]
        l_ref[...] = alpha * l_ref[...] + jnp.sum(functools.reduce(jnp.add, ps), axis=1, keepdims=True)
        m_ref[...] = m_new
        pv = functools.reduce(jnp.add, [
            lax.dot_general(p.astype(BF16), vb, nt if keys_on_lanes else nn, preferred_element_type=F32)
            for p, vb in zip(ps, vbs)])
        acc_ref[...] = alpha * acc_ref[...] + pv

    thr = thr_ref[0]
    attend([r[0].astype(BF16) for r in k_refs], [r[0].astype(BF16) for r in v_refs],
           [sc_ref[0, s_id * pg + p] >= thr for p in range(pg)], True)

    @pl.when(s_id == n_steps - 1)
    def _():
        pad = jnp.zeros((PAGE_SIZE - ds, ATTN_DIM), BF16)
        lane = lax.broadcasted_iota(I32, (ds, PAGE_SIZE), 1)
        trow = lax.broadcasted_iota(I32, (ds, PAGE_SIZE), 0)
        sel = (sc_ref[0, n_pages] >= thr) & (lane <= trow)
        attend([jnp.concatenate([kn_ref[0], pad], axis=0)], [jnp.concatenate([vn_ref[0], pad], axis=0)],
               [sel], False)
        o = jnp.where(head_of_row == head_of_col, acc_ref[...] / l_ref[...], 0.0)
        out = o[0:ds]
        for h in range(1, N_HEADS):
            out = out + o[h * ds:(h + 1) * ds]
        out_ref[0] = out.astype(BF16)


def _attn_sample(page_table_flat, k_pages, v_pages, q_s, k_new_b, v_new_b, scores, thr, *, pg, n_pages, ds):
    DB = q_s.shape[0]
    page_spec = lambda p: pl.BlockSpec((1, ATTN_DIM, PAGE_SIZE),
                                       lambda b, s, pt: (pt[b * n_pages + s * pg + p], 0, 0))
    per_seq = lambda shp: pl.BlockSpec((1,) + shp, lambda b, s, pt: (b,) + (0,) * len(shp))
    hq = N_HEADS * ds
    grid_spec = pltpu.PrefetchScalarGridSpec(
        num_scalar_prefetch=1, grid=(DB, n_pages // pg),
        in_specs=[page_spec(p) for p in range(pg)] * 2
        + [per_seq((ds, ATTN_DIM))] * 3 + [per_seq((n_pages + 1, ds, PAGE_SIZE)), per_seq((ds, PAGE_SIZE))],
        out_specs=per_seq((ds, ATTN_DIM)),
        scratch_shapes=[pltpu.VMEM((hq, ATTN_DIM), BF16), pltpu.VMEM((hq, ATTN_DIM), F32),
                        pltpu.VMEM((hq, 1), F32), pltpu.VMEM((hq, 1), F32)])
    return pl.pallas_call(
        functools.partial(_attn_sample_kernel, pg=pg, n_pages=n_pages, ds=ds),
        grid_spec=grid_spec, out_shape=jax.ShapeDtypeStruct((DB, ds, ATTN_DIM), BF16),
        compiler_params=_cparams(2), name="attn_sample",
    )(page_table_flat, *([k_pages] * pg), *([v_pages] * pg), q_s, k_new_b, v_new_b, scores, thr)


def _rope_tables(pos):
    half = HEAD_DIM // 2
    inv = ROPE_THETA ** (-jnp.arange(half, dtype=F32) / half)
    ang = pos.astype(F32)[:, None] * inv[None, :]
    cos, sin = jnp.cos(ang), jnp.sin(ang)
    return jnp.tile(cos, (1, 4)), jnp.tile(jnp.concatenate([-sin, sin], axis=1), (1, 2))


def _pick(n, prefs):
    for p in prefs:
        if n % p == 0:
            return p
    raise ValueError(f"no supported tile for extent {n}")


def kernel(x_prompt, x_sample, cache_k, cache_v, cache_kidx, state_conv, page_table, norm_attn_pre,
           norm_attn_post, w_in, w_dw, b_dw, conv_ln_g, conv_ln_b, w_out, norm_mlp_pre, norm_mlp_post,
           w_up, w_down):
    B, S, _ = x_prompt.shape
    DB, DS, _ = x_sample.shape
    n_pages = page_table.shape[1]
    past_len = n_pages * PAGE_SIZE
    assert w_in.shape[0] == 1, "single-layer kernel"
    assert DS == SUBLANES, "sample rows per sequence must fill one sublane tile"

    w_in_b = jnp.pad(w_in[0], ((0, 0), (0, D_IN_PAD - D_IN))).astype(BF16)
    w_out_b, w_up_b, w_dn_b = w_out[0].astype(BF16), w_up[0].astype(BF16), w_down[0].astype(BF16)
    g_pre, g_post = norm_attn_pre, norm_attn_post
    g_mpre, g_mpost = norm_mlp_pre, norm_mlp_post

    tm = _pick(S, (256, 128))
    tk = _pick(S, (256, 128))
    tq = _pick(S, (256, 128))
    cos_p, sin_p = _rope_tables(jnp.arange(S, dtype=I32))
    assert tm == tq, "the front kernel lays qi out per attention query tile"
    (kT_p, vT_p, kiT_p, k4, vT4, qT, qi2, wT, ki4, kn2, conv_p, state_p) = _front_prompt(
        x_prompt, g_pre, w_in_b, cos_p, sin_p, w_dw[0], b_dw, conv_ln_g, conv_ln_b, tm=tm, tk=tk)
    attn_p = _attn_prompt(qT, qi2, wT, k4, vT4, ki4, kn2, tq=tq, tk=tk)
    y_p = _back(x_prompt.reshape(B * S, D_MODEL), conv_p.reshape(B * S, C_CONV),
                attn_p.reshape(B * S, ATTN_DIM), w_out_b, g_post, g_mpre, g_mpost, w_up_b, w_dn_b,
                tm=tm).reshape(B, S, D_MODEL)

    R = DB * DS
    gs = _pick(DB, (32, 16, 8, 4, 2, 1))
    cos_s, sin_s = _rope_tables(past_len + jnp.arange(DS, dtype=I32))
    cos_s, sin_s = jnp.tile(cos_s, (gs, 1)), jnp.tile(sin_s, (gs, 1))
    (k_s, v_s, ki_s, q_s, qi_s, w_s, conv_s, state_s) = _front_sample(
        x_sample.reshape(R, D_MODEL), g_pre, w_in_b, cos_s, sin_s, w_dw[0], b_dw, conv_ln_g, conv_ln_b,
        state_conv[0], gs=gs, ds=DS)
    pg = _pick(n_pages, (16, 8, 4, 2, 1))
    pt_flat = page_table.reshape(-1)
    qi_st = qi_s.reshape(DB, DS, H_IDX, D_IDX).transpose(0, 2, 1, 3).reshape(DB, H_IDX * DS, D_IDX)
    w_sq = w_s[:, D_IDX:D_IDX + H_IDX].reshape(DB, DS, H_IDX)
    n_phys = cache_k.shape[1]
    pagesT = lambda c: jnp.transpose(c[0], (0, 2, 3, 1)).reshape(n_phys, ATTN_DIM, PAGE_SIZE)
    scores, thr = _idx_sample(pt_flat, jnp.transpose(cache_kidx[0], (0, 2, 1)), qi_st, w_sq,
                              ki_s.astype(BF16).reshape(DB, DS, D_IDX),
                              pg=_pick(n_pages, (64, 32, 16, 8, 4, 2, 1)), n_pages=n_pages, ds=DS)
    attn_s = _attn_sample(pt_flat, pagesT(cache_k), pagesT(cache_v),
                          q_s.reshape(DB, DS, ATTN_DIM), k_s.astype(BF16).reshape(DB, DS, ATTN_DIM),
                          v_s.astype(BF16).reshape(DB, DS, ATTN_DIM), scores, thr,
                          pg=pg, n_pages=n_pages, ds=DS)
    y_s = _back(x_sample.reshape(R, D_MODEL), conv_s, attn_s.reshape(R, ATTN_DIM), w_out_b, g_post, g_mpre,
                g_mpost, w_up_b, w_dn_b, tm=_pick(R, (256, 128, 64, 32, 16, 8))).reshape(DB, DS, D_MODEL)

    hd = (N_HEADS, HEAD_DIM)
    unT = lambda t: jnp.transpose(t.reshape(B, N_HEADS, HEAD_DIM, S), (0, 3, 1, 2))[None]
    return (y_p, y_s,
            unT(kT_p), unT(vT_p), jnp.transpose(kiT_p, (0, 2, 1))[None], state_p[None],
            k_s.reshape(1, DB, DS, *hd), v_s.reshape(1, DB, DS, *hd), ki_s.reshape(1, DB, DS, D_IDX),
            state_s[None])
```

```python
import functools

import jax
import jax.numpy as jnp
import numpy as np
from jax import lax
from jax.experimental import pallas as pl
from jax.experimental.pallas import tpu as pltpu

F32 = jnp.float32
BF16 = jnp.bfloat16
I32 = jnp.int32

D_MODEL = 1024
C_CONV = 512
CONV_WIDTH = 31
CONV_HALO = CONV_WIDTH - 1
N_HEADS = 8
HEAD_DIM = 64
ATTN_DIM = N_HEADS * HEAD_DIM
H_IDX = 8
D_IDX = 64
TOPK_MAX = 256
D_FF = 4 * D_MODEL
ROPE_THETA = 10000.0
EPS = 1e-6
PAGE_SIZE = 128

COL_A, COL_G, COL_Q, COL_K, COL_V, COL_QI, COL_KW = 0, 512, 1024, 1536, 2048, 2560, 3072
D_IN = 3144
LANES = 128
SUBLANES = 8
D_IN_PAD = COL_KW + LANES

INT_MIN = -(2 ** 31)
F32_INF_BITS = 0x7F800000
NEG_BIG = -1e30
BF16_ROWS = 16
V_ROWS = HEAD_DIM + BF16_ROWS
LOG2_E = 1.4426950408889634
MAX_RAW_LOGIT = 64.0
NORM_SLACK = 1.05

VMEM_LIMIT = 56 * 1024 * 1024


def _cparams(n_axes):
    return pltpu.CompilerParams(dimension_semantics=("arbitrary",) * n_axes,
                                vmem_limit_bytes=VMEM_LIMIT)


def _rms(x):
    return x * lax.rsqrt(jnp.mean(x * x, axis=-1, keepdims=True) + EPS)


def _rope_slab(x, cos, sin_signed, first_half):
    outs = []
    for c in range(x.shape[1] // LANES):
        xc = x[:, c * LANES:(c + 1) * LANES]
        partner = jnp.where(first_half, pltpu.roll(xc, LANES - 32, 1), pltpu.roll(xc, 32, 1))
        outs.append(xc * cos + partner * sin_signed)
    return outs[0] if len(outs) == 1 else jnp.concatenate(outs, axis=1)


def _project(x, g, w_ref, cos, sin_signed):
    hb = (_rms(x) * g).astype(BF16)
    dot = lambda c0, n: jnp.dot(hb, w_ref[:, c0:c0 + n], preferred_element_type=F32)
    first_half = (lax.broadcasted_iota(I32, (x.shape[0], LANES), 1) % 64) < 32
    a = dot(COL_A, C_CONV)
    gate = dot(COL_G, C_CONV)
    q = _rope_slab(dot(COL_Q, ATTN_DIM), cos, sin_signed, first_half) * (HEAD_DIM ** -0.5 * LOG2_E)
    k = _rope_slab(dot(COL_K, ATTN_DIM), cos, sin_signed, first_half)
    v = dot(COL_V, ATTN_DIM)
    qi = _rope_slab(dot(COL_QI, H_IDX * D_IDX), cos, sin_signed, first_half)
    kw = dot(COL_KW, LANES)
    ki_slab = _rope_slab(kw, cos, sin_signed, first_half)
    w_eff = kw * ((H_IDX ** -0.5) * (D_IDX ** -0.5))
    return a, gate, q, k, v, qi, ki_slab, w_eff


def _glu(a, gate):
    return a * (1.0 / (1.0 + jnp.exp(-gate)))


def _ln_swish(y, ln_g, ln_b):
    mu = jnp.mean(y, axis=-1, keepdims=True)
    d = y - mu
    var = jnp.mean(d * d, axis=-1, keepdims=True)
    y = (d * lax.rsqrt(var + EPS)) * ln_g + ln_b
    return y * (1.0 / (1.0 + jnp.exp(-y)))


def _key_to_f32(c):
    c = jnp.maximum(c, -F32_INF_BITS)
    bits = jnp.where(c >= 0, c, INT_MIN - c)
    return lax.bitcast_convert_type(bits, F32)


def _f32_to_key(x):
    bits = lax.bitcast_convert_type(x, I32)
    return jnp.where(bits >= 0, bits, INT_MIN - bits)


def _bisect_threshold(count_ge, lo, hi, topk):
    def still_open(lo, hi, cnt):
        return (cnt != topk) & (hi > lo + 1)

    def step(lo, hi, cnt):
        act = still_open(lo, hi, cnt)
        mid = jnp.right_shift(lo, 1) + jnp.right_shift(hi, 1) + (lo & hi & 1)
        c = count_ge(_key_to_f32(mid))
        take = act & (c >= topk)
        drop = act & (c < topk)
        return jnp.where(take, mid, lo), jnp.where(drop, mid, hi), jnp.where(take, c, cnt)

    def body(st):
        lo, hi, cnt, _ = st
        lo, hi, cnt = step(*step(lo, hi, cnt))
        return lo, hi, cnt, jnp.max(jnp.where(still_open(lo, hi, cnt), 1.0, 0.0)) > 0.0

    init = (lo, hi, jnp.full(lo.shape, jnp.inf, F32), jnp.bool_(True))
    lo, _, cnt, _ = lax.while_loop(lambda st: st[3], body, init)
    return _key_to_f32(lo), cnt


def _bisect_tie_index(count_eq_below, need, shape, nbits):
    def body(b, c):
        cand = c + lax.shift_left(jnp.int32(1), jnp.int32(nbits - 1) - b)
        return jnp.where(count_eq_below(cand) < need, cand, c)
    return lax.fori_loop(0, nbits, body, jnp.zeros(shape, I32))


def _front_prompt_kernel(x_ref, g_ref, w_ref, cos_ref, sin_ref, wdw_ref, bdw_ref, lng_ref, lnb_ref,
                         kT_out, vT_out, kiT_out, kb_out, vTb_out, qT_out, qi2_out, wT_out, kib_out, kn2_out,
                         conv_out, state_out, ext_ref, sh_ref, *, tm, tk, conv_rows):
    j = pl.program_id(1)
    a, gate, q, k, v, qi, ki_slab, w_eff = _project(x_ref[0], g_ref[...], w_ref, cos_ref[...], sin_ref[...])

    kT = k.T
    kT_out[0] = kT
    kn2 = jnp.max(jnp.sum((kT * kT).reshape(N_HEADS, HEAD_DIM, tm), axis=1), axis=1, keepdims=True)
    kn2_out[0, 0] = jnp.broadcast_to(kn2, (N_HEADS, LANES))
    vT = v.T
    vT_out[0] = vT
    kiT_out[0] = ki_slab.T[:D_IDX]
    kb = k.astype(BF16)
    vTb = vT.astype(BF16)
    ones = jnp.ones((V_ROWS - HEAD_DIM, tm), BF16)
    vTx = jnp.concatenate([blk for h in range(N_HEADS)
                           for blk in (vTb[h * HEAD_DIM:(h + 1) * HEAD_DIM], ones)], axis=0)
    kib = ki_slab[:, :D_IDX].astype(BF16)
    for c in range(tm // tk):
        kb_out[0, c] = kb[c * tk:(c + 1) * tk]
        vTb_out[0, c] = vTx[:, c * tk:(c + 1) * tk]
        kib_out[0, c] = kib[c * tk:(c + 1) * tk]
    qT_out[0] = q.T.astype(BF16)
    qiT = qi.T.astype(BF16)
    qi2_out[0, 0] = jnp.concatenate([qiT[h * D_IDX:(h + 1) * D_IDX] for h in range(H_IDX)], axis=1)
    wT_out[0] = w_eff.T[D_IDX:D_IDX + H_IDX]

    @pl.when(j == 0)
    def _():
        ext_ref[0:32, :] = jnp.zeros((32, C_CONV), F32)
    ext_ref[32:32 + tm, :] = _glu(a, gate)
    for r in range(1, SUBLANES):
        sh_ref[r - 1] = ext_ref[r:r + tm + 24, :]
    for r0 in range(0, tm, conv_rows):
        acc = jnp.zeros((conv_rows, C_CONV), F32)
        for t in range(CONV_WIDTH):
            lo = r0 + 32 - CONV_HALO + t
            r = lo % SUBLANES
            src = ext_ref if r == 0 else sh_ref.at[r - 1]
            acc = acc + wdw_ref[t:t + 1, :] * src[lo - r:lo - r + conv_rows, :]
        y = _ln_swish(acc + bdw_ref[...], lng_ref[...], lnb_ref[...])
        conv_out[0, r0:r0 + conv_rows, :] = y.astype(BF16)
    state_out[0] = ext_ref[32 + tm - CONV_HALO:32 + tm, :]
    ext_ref[0:32, :] = ext_ref[tm:tm + 32, :]


def _front_prompt(x, g_pre, w_in_b, cos, sin_signed, w_dw, b_dw, ln_g, ln_b, *, tm, tk):
    B, S, _ = x.shape
    n_c = S // tk
    row = lambda w: pl.BlockSpec((1, tm, w), lambda b, j: (b, j, 0))
    const = lambda shp: pl.BlockSpec(shp, lambda b, j: (0,) * len(shp))
    chunked = lambda d2, d3: pl.BlockSpec((1, tm // tk, d2, d3), lambda b, j: (b, j, 0, 0))
    colT = lambda r: pl.BlockSpec((1, r, tm), lambda b, j: (b, 0, j))
    out_shape = (
        jax.ShapeDtypeStruct((B, ATTN_DIM, S), F32),
        jax.ShapeDtypeStruct((B, ATTN_DIM, S), F32),
        jax.ShapeDtypeStruct((B, D_IDX, S), F32),
        jax.ShapeDtypeStruct((B, n_c, tk, ATTN_DIM), BF16),
        jax.ShapeDtypeStruct((B, n_c, N_HEADS * V_ROWS, tk), BF16),
        jax.ShapeDtypeStruct((B, ATTN_DIM, S), BF16),
        jax.ShapeDtypeStruct((B, S // tm, D_IDX, H_IDX * tm), BF16),
        jax.ShapeDtypeStruct((B, H_IDX, S), F32),
        jax.ShapeDtypeStruct((B, n_c, tk, D_IDX), BF16),
        jax.ShapeDtypeStruct((B, S // tm, N_HEADS, LANES), F32),
        jax.ShapeDtypeStruct((B, S, C_CONV), BF16),
        jax.ShapeDtypeStruct((B, CONV_HALO, C_CONV), F32),
    )
    out_specs = (colT(ATTN_DIM), colT(ATTN_DIM), colT(D_IDX), chunked(tk, ATTN_DIM),
                 chunked(N_HEADS * V_ROWS, tk), colT(ATTN_DIM),
                 pl.BlockSpec((1, 1, D_IDX, H_IDX * tm), lambda b, j: (b, j, 0, 0)),
                 colT(H_IDX), chunked(tk, D_IDX),
                 pl.BlockSpec((1, 1, N_HEADS, LANES), lambda b, j: (b, j, 0, 0)), row(C_CONV),
                 pl.BlockSpec((1, CONV_HALO, C_CONV), lambda b, j: (b, 0, 0)))
    in_specs = [row(D_MODEL), const((1, D_MODEL)), const((D_MODEL, D_IN_PAD)),
                pl.BlockSpec((tm, LANES), lambda b, j: (j, 0)), pl.BlockSpec((tm, LANES), lambda b, j: (j, 0)),
                const((CONV_WIDTH, C_CONV)), const((1, C_CONV)), const((1, C_CONV)), const((1, C_CONV))]
    return pl.pallas_call(
        functools.partial(_front_prompt_kernel, tm=tm, tk=tk, conv_rows=64),
        grid=(B, S // tm), in_specs=in_specs, out_specs=out_specs, out_shape=out_shape,
        scratch_shapes=[pltpu.VMEM((tm + 32, C_CONV), F32),
                        pltpu.VMEM((SUBLANES - 1, tm + 24, C_CONV), F32)],
        compiler_params=_cparams(2), name="front_prompt",
    )(x, g_pre, w_in_b, cos, sin_signed, w_dw, b_dw, ln_g, ln_b)


def _attn_prompt_kernel(qT_ref, qi2_ref, wT_ref, k_ref, vT_ref, ki_ref, kn2_ref, out_ref,
                        sc_ref, gm_ref, qz_ref, acc_ref, m_ref, *, tq, tk, topk, idx_bits):
    i = pl.program_id(1)
    q0 = i * tq
    n_diag = tq // tk
    n_full = i * n_diag
    n_chunks = n_full + n_diag
    n_rg = tk // SUBLANES
    row_iota = lax.broadcasted_iota(I32, (tk, tq), 0)
    qpos = q0 + lax.broadcasted_iota(I32, (tk, tq), 1)

    def over_chunks(stage1, stage2):
        def pair(g, carry):
            a0, a1 = stage1(2 * g, False), stage1(2 * g + 1, False)
            stage2(2 * g, a0, False)
            stage2(2 * g + 1, a1, False)
            return carry
        lax.fori_loop(0, lax.shift_right_logical(n_full, 1), pair, 0)

        @pl.when(lax.bitwise_and(n_full, 1) == 1)
        def _():
            stage2(n_full - 1, stage1(n_full - 1, False), False)
        for dch in range(n_diag):
            stage2(n_full + dch, stage1(n_full + dch, True), True)

    def score_dots(c, causal):
        return jnp.dot(ki_ref[0, c], qi2_ref[0, 0], preferred_element_type=F32)

    def score_reduce(c, d, causal):
        s = jnp.zeros((tk, tq), F32)
        for h in range(H_IDX):
            s = s + wT_ref[0, h:h + 1, :] * jnp.maximum(d[:, h * tq:(h + 1) * tq], 0.0)
        if causal:
            s = jnp.where(c * tk + row_iota <= qpos, s, -jnp.inf)
        sc_ref[c] = s
        gm_ref[...] = jnp.maximum(gm_ref[...], s)

    gm_ref[...] = jnp.full((tk, tq), -jnp.inf, F32)
    over_chunks(score_dots, score_reduce)

    @pl.when(lax.bitwise_and(n_chunks, 1) == 1)
    def _():
        sc_ref[n_chunks] = jnp.full((tk, tq), -jnp.inf, F32)

    def count(pred):
        def body(g, acc):
            for u in range(2):
                c = 2 * g + u
                m = jnp.where(pred(c, sc_ref[c]), 1.0, 0.0).reshape(n_rg, SUBLANES, tq)
                acc = acc + jnp.sum(m, axis=0)
            return acc
        acc = lax.fori_loop(0, lax.shift_right_logical(n_chunks + 1, 1), body, jnp.zeros((SUBLANES, tq), F32))
        return jnp.sum(acc, axis=0, keepdims=True)

    gm = gm_ref[...]
    hi_key = _f32_to_key(jnp.max(gm, axis=0, keepdims=True)) + 1
    lo_val = jnp.min(gm, axis=0, keepdims=True) if tk >= topk else jnp.full((1, tq), -jnp.inf, F32)
    thr, n_ge = _bisect_threshold(lambda t: count(lambda c, s: s >= t), _f32_to_key(lo_val), hi_key, topk)
    unknown = (n_ge == jnp.inf) & (thr > -jnp.inf)
    n_ge = lax.cond(jnp.max(jnp.where(unknown, 1.0, 0.0)) > 0.0,
                    lambda: jnp.where(unknown, count(lambda c, s: s >= thr), n_ge), lambda: n_ge)
    tied = (n_ge > topk) & (thr > -jnp.inf)

    @pl.when(jnp.max(jnp.where(tied, 1.0, 0.0)) > 0.0)
    def _():
        need = topk - count(lambda c, s: s > thr)
        last = _bisect_tie_index(
            lambda cand: count(lambda c, s: (s == thr) & (c * tk + row_iota < cand)),
            need, (1, tq), idx_bits)

        def drop(c, carry):
            s = sc_ref[c]
            sc_ref[c] = jnp.where(tied & (s == thr) & (c * tk + row_iota > last), -jnp.inf, s)
            return carry
        lax.fori_loop(0, n_chunks, drop, 0)

    zeros_half = jnp.zeros((HEAD_DIM, tq), BF16)
    for pr in range(N_HEADS // 2):
        q_even = qT_ref[0, (2 * pr) * HEAD_DIM:(2 * pr + 1) * HEAD_DIM, :]
        q_odd = qT_ref[0, (2 * pr + 1) * HEAD_DIM:(2 * pr + 2) * HEAD_DIM, :]
        qz_ref[pr] = jnp.concatenate([jnp.concatenate([q_even, zeros_half], axis=1),
                                      jnp.concatenate([zeros_half, q_odd], axis=1)], axis=0)
    m_ref[...] = jnp.full((N_HEADS, SUBLANES, tq), NEG_BIG, F32)
    acc_ref[...] = jnp.zeros((N_HEADS * V_ROWS, tq), F32)

    qf = qT_ref[0].astype(F32)
    qn2 = jnp.sum((qf * qf).reshape(N_HEADS, HEAD_DIM, tq), axis=1)
    kn2 = jnp.max(kn2_ref[0], axis=0)[:, 0:1]
    small_logits = jnp.max(qn2 * kn2) * NORM_SLACK <= MAX_RAW_LOGIT ** 2

    def masked_bias(c, causal):
        sel = sc_ref[c] >= thr
        if causal:
            sel = sel & (c * tk + row_iota <= qpos)
        return jnp.where(sel, 0.0, -jnp.inf)

    def raw_probs(c, causal):
        neg = masked_bias(c, causal)
        probs = []
        for pr in range(N_HEADS // 2):
            d = jnp.dot(k_ref[0, c, :, pr * LANES:(pr + 1) * LANES], qz_ref[pr], preferred_element_type=F32)
            probs += [jnp.exp2(d[:, e * tq:(e + 1) * tq] + neg).astype(BF16) for e in range(2)]
        return probs

    def raw_accumulate(c, probs, causal):
        for h in range(N_HEADS):
            rows = slice(h * V_ROWS, (h + 1) * V_ROWS)
            acc_ref[rows, :] += jnp.dot(vT_ref[0, c, rows, :], probs[h], preferred_element_type=F32)

    def attend_chunk(c, _, causal):
        neg = masked_bias(c, causal)
        masked, col_max = [], []
        for pr in range(N_HEADS // 2):
            d = jnp.dot(k_ref[0, c, :, pr * LANES:(pr + 1) * LANES], qz_ref[pr], preferred_element_type=F32)
            for e in range(2):
                s = (d[:, e * tq:(e + 1) * tq] + neg).reshape(n_rg, SUBLANES, tq)
                masked.append(s)
                col_max.append(jnp.max(jnp.max(s, axis=0), axis=0, keepdims=True))
        for h in range(N_HEADS):
            s = masked[h]
            m_old = m_ref[h]
            m_new = jnp.maximum(m_old, col_max[h])
            alpha = jnp.exp2(m_old - m_new)
            p = jnp.exp2(s - m_new[None]).reshape(tk, tq).astype(BF16)
            m_ref[h] = m_new
            rows = slice(h * V_ROWS, (h + 1) * V_ROWS)
            pv = jnp.dot(vT_ref[0, c, rows, :], p, preferred_element_type=F32)
            acc = acc_ref[rows, :].reshape(V_ROWS // SUBLANES, SUBLANES, tq) * alpha[None]
            acc_ref[rows, :] = acc.reshape(V_ROWS, tq) + pv

    @pl.when(small_logits)
    def _():
        over_chunks(raw_probs, raw_accumulate)

    @pl.when(jnp.logical_not(small_logits))
    def _():
        over_chunks(lambda c, causal: None, attend_chunk)

    outs = []
    for h in range(N_HEADS):
        denom = acc_ref[h * V_ROWS + HEAD_DIM:h * V_ROWS + HEAD_DIM + 1, :]
        outs.append(acc_ref[h * V_ROWS:h * V_ROWS + HEAD_DIM, :] / denom)
    out_ref[0] = jnp.concatenate(outs, axis=0).T.astype(BF16)


def _attn_prompt(qT, qi2, wT, k4, vT4, ki4, kn2, *, tq, tk):
    B, n_c = k4.shape[:2]
    S = n_c * tk
    topk = min(TOPK_MAX, S // 4)
    whole = lambda shp: pl.BlockSpec((1,) + shp, lambda b, i: (b,) + (0,) * len(shp))
    colT = lambda r: pl.BlockSpec((1, r, tq), lambda b, i: (b, 0, i))
    return pl.pallas_call(
        functools.partial(_attn_prompt_kernel, tq=tq, tk=tk, topk=topk, idx_bits=int(S).bit_length()),
        grid=(B, S // tq),
        in_specs=[colT(ATTN_DIM), pl.BlockSpec((1, 1, D_IDX, H_IDX * tq), lambda b, i: (b, i, 0, 0)), colT(H_IDX),
                  whole((n_c, tk, ATTN_DIM)), whole((n_c, N_HEADS * V_ROWS, tk)), whole((n_c, tk, D_IDX)),
                  whole(kn2.shape[1:])],
        out_specs=pl.BlockSpec((1, tq, ATTN_DIM), lambda b, i: (b, i, 0)),
        out_shape=jax.ShapeDtypeStruct((B, S, ATTN_DIM), BF16),
        scratch_shapes=[pltpu.VMEM((n_c + 1, tk, tq), F32),
                        pltpu.VMEM((tk, tq), F32),
                        pltpu.VMEM((N_HEADS // 2, LANES, 2 * tq), BF16),
                        pltpu.VMEM((N_HEADS * V_ROWS, tq), F32),
                        pltpu.VMEM((N_HEADS, SUBLANES, tq), F32)],
        compiler_params=_cparams(2), name="attn_prompt",
    )(qT, qi2, wT, k4, vT4, ki4, kn2)


def _back_kernel(x_ref, c_ref, a_ref, wo_ref, gpost_ref, gmpre_ref, gmpost_ref, wup_ref, wdn_ref, y_ref,
                 *, ff_chunk):
    mix = (jnp.dot(c_ref[...], wo_ref[0:C_CONV, :], preferred_element_type=F32)
           + jnp.dot(a_ref[...], wo_ref[C_CONV:, :], preferred_element_type=F32))
    x1 = x_ref[...] + _rms(mix) * gpost_ref[...]
    hm = (_rms(x1) * gmpre_ref[...]).astype(BF16)
    m = jnp.zeros_like(x1)
    for f in range(0, D_FF, ff_chunk):
        up = jnp.dot(hm, wup_ref[:, f:f + ff_chunk], preferred_element_type=F32)
        act = jnp.square(jnp.maximum(up, 0.0)).astype(BF16)
        m = m + jnp.dot(act, wdn_ref[f:f + ff_chunk, :], preferred_element_type=F32)
    y_ref[...] = x1 + _rms(m) * gmpost_ref[...]


def _back(x2, conv2, attn2, w_out_b, g_post, g_mpre, g_mpost, w_up_b, w_dn_b, *, tm):
    R = x2.shape[0]
    row = lambda w: pl.BlockSpec((tm, w), lambda r: (r, 0))
    const = lambda shp: pl.BlockSpec(shp, lambda r: (0,) * len(shp))
    return pl.pallas_call(
        functools.partial(_back_kernel, ff_chunk=1024),
        grid=(R // tm,),
        in_specs=[row(D_MODEL), row(C_CONV), row(ATTN_DIM), const((D_MODEL, D_MODEL)),
                  const((1, D_MODEL)), const((1, D_MODEL)), const((1, D_MODEL)),
                  const((D_MODEL, D_FF)), const((D_FF, D_MODEL))],
        out_specs=row(D_MODEL), out_shape=jax.ShapeDtypeStruct((R, D_MODEL), F32),
        compiler_params=_cparams(1), name="back",
    )(x2, conv2, attn2, w_out_b, g_post, g_mpre, g_mpost, w_up_b, w_dn_b)


def _front_sample_kernel(x_ref, g_ref, w_ref, cos_ref, sin_ref, wdw_ref, bdw_ref, lng_ref, lnb_ref, st_ref,
                         k_out, v_out, ki_out, q_out, qi_out, w_out, conv_out, state_out, ext_ref, *, gs, ds):
    a, gate, q, k, v, qi, ki_slab, w_eff = _project(x_ref[...], g_ref[...], w_ref, cos_ref[...], sin_ref[...])
    k_out[...] = k
    v_out[...] = v
    ki_out[...] = ki_slab[:, :D_IDX]
    q_out[...] = q.astype(BF16)
    qi_out[...] = qi.astype(BF16)
    w_out[...] = w_eff

    ext_ref[:, 0:CONV_HALO, :] = st_ref[...]
    ext_ref[:, CONV_HALO:CONV_HALO + ds, :] = _glu(a, gate).reshape(gs, ds, C_CONV)
    acc = jnp.zeros((gs, ds, C_CONV), F32)
    for t in range(CONV_WIDTH):
        acc = acc + wdw_ref[t:t + 1, :] * ext_ref[:, t:t + ds, :]
    y = _ln_swish(acc.reshape(gs * ds, C_CONV) + bdw_ref[...], lng_ref[...], lnb_ref[...])
    conv_out[...] = y.astype(BF16)
    state_out[...] = ext_ref[:, ds:ds + CONV_HALO, :]


def _front_sample(x2, g_pre, w_in_b, cos, sin_signed, w_dw, b_dw, ln_g, ln_b, state, *, gs, ds):
    R = x2.shape[0]
    DB = R // ds
    tm = gs * ds
    row = lambda w: pl.BlockSpec((tm, w), lambda r: (r, 0))
    const = lambda shp: pl.BlockSpec(shp, lambda r: (0,) * len(shp))
    st_spec = pl.BlockSpec((gs, CONV_HALO, C_CONV), lambda r: (r, 0, 0))
    out_shape = (
        jax.ShapeDtypeStruct((R, ATTN_DIM), F32), jax.ShapeDtypeStruct((R, ATTN_DIM), F32),
        jax.ShapeDtypeStruct((R, D_IDX), F32),
        jax.ShapeDtypeStruct((R, ATTN_DIM), BF16), jax.ShapeDtypeStruct((R, H_IDX * D_IDX), BF16),
        jax.ShapeDtypeStruct((R, LANES), F32),
        jax.ShapeDtypeStruct((R, C_CONV), BF16), jax.ShapeDtypeStruct((DB, CONV_HALO, C_CONV), F32))
    out_specs = (row(ATTN_DIM), row(ATTN_DIM), row(D_IDX), row(ATTN_DIM), row(H_IDX * D_IDX), row(LANES),
                 row(C_CONV), st_spec)
    return pl.pallas_call(
        functools.partial(_front_sample_kernel, gs=gs, ds=ds),
        grid=(R // tm,),
        in_specs=[row(D_MODEL), const((1, D_MODEL)), const((D_MODEL, D_IN_PAD)),
                  const((tm, LANES)), const((tm, LANES)),
                  const((CONV_WIDTH, C_CONV)), const((1, C_CONV)), const((1, C_CONV)), const((1, C_CONV)),
                  st_spec],
        out_specs=out_specs, out_shape=out_shape,
        scratch_shapes=[pltpu.VMEM((gs, CONV_HALO + ds + 2, C_CONV), F32)],
        compiler_params=_cparams(1), name="front_sample",
    )(x2, g_pre, w_in_b, cos, sin_signed, w_dw, b_dw, ln_g, ln_b, state)


def _idx_sample_kernel(pt_ref, *refs, pg, n_pages, ds, gb, topk, idx_bits):
    page_refs = refs[:pg]
    qi_ref, w_ref, kin_ref, sc_ref, thr_ref, wb_ref = refs[pg:]
    s_id = pl.program_id(1)
    bb = pl.program_id(0) % gb
    n_steps = n_pages // pg
    lane = lax.broadcasted_iota(I32, (ds, PAGE_SIZE), 1)
    trow = lax.broadcasted_iota(I32, (ds, PAGE_SIZE), 0)

    @pl.when(s_id == 0)
    def _():
        for h in range(H_IDX):
            wb_ref[h] = jnp.broadcast_to(w_ref[0, :, h:h + 1], (ds, PAGE_SIZE))

    def scores(keys_b, keys_on_lanes):
        contract = (((1,), (0,)), ((), ())) if keys_on_lanes else (((1,), (1,)), ((), ()))
        d = lax.dot_general(qi_ref[0], keys_b, contract, preferred_element_type=F32)
        s = jnp.zeros((ds, PAGE_SIZE), F32)
        for h in range(H_IDX):
            s = s + wb_ref[h] * jnp.maximum(d[h * ds:(h + 1) * ds], 0.0)
        return s

    for p in range(pg):
        sc_ref[bb, s_id * pg + p] = scores(page_refs[p][0].astype(BF16), True)

    @pl.when(s_id == n_steps - 1)
    def _():
        kin = jnp.concatenate([kin_ref[0], jnp.zeros((PAGE_SIZE - ds, D_IDX), BF16)], axis=0)
        sc_ref[bb, n_pages] = jnp.where(lane <= trow, scores(kin, False), -jnp.inf)

    @pl.when((s_id == n_steps - 1) & (bb == gb - 1))
    def _():
        shape4 = (gb, n_pages + 1, ds, PAGE_SIZE)

        def count(pred):
            hits = jnp.where(pred(sc_ref[...]), 1.0, 0.0)
            return jnp.sum(jnp.sum(hits, axis=1, keepdims=True), axis=3, keepdims=True)

        shape = (gb, 1, ds, 1)
        thr, n_ge = _bisect_threshold(lambda t: count(lambda s: s >= t), jnp.full(shape, -F32_INF_BITS, I32),
                                      jnp.full(shape, F32_INF_BITS, I32), topk)
        tied = (n_ge > topk) & (thr > -jnp.inf)

        @pl.when(jnp.max(jnp.where(tied, 1.0, 0.0)) > 0.0)
        def _():
            key_idx = (lax.broadcasted_iota(I32, shape4, 1) * PAGE_SIZE
                       + lax.broadcasted_iota(I32, shape4, 3))
            need = topk - count(lambda s: s > thr)
            last = _bisect_tie_index(lambda cand: count(lambda s: (s == thr) & (key_idx < cand)),
                                     need, (gb, 1, ds, 1), idx_bits)
            s = sc_ref[...]
            sc_ref[...] = jnp.where(tied & (s == thr) & (key_idx > last), -jnp.inf, s)

        thr_ref[...] = jnp.broadcast_to(thr.reshape(gb, ds, 1), (gb, ds, PAGE_SIZE))


def _idx_sample(page_table_flat, kidx_pages, qi_st, w_s, ki_new_b, *, pg, n_pages, ds):
    DB = qi_st.shape[0]
    gb = _pick(DB, (16, 8, 4, 2, 1))
    L = n_pages * PAGE_SIZE + ds
    topk = min(TOPK_MAX, L // 4)
    page_spec = lambda p: pl.BlockSpec((1, D_IDX, PAGE_SIZE),
                                       lambda b, s, pt: (pt[b * n_pages + s * pg + p], 0, 0))
    per_seq = lambda shp: pl.BlockSpec((1,) + shp, lambda b, s, pt: (b,) + (0,) * len(shp))
    per_group = lambda shp: pl.BlockSpec((gb,) + shp, lambda b, s, pt: (b // gb,) + (0,) * len(shp))
    grid_spec = pltpu.PrefetchScalarGridSpec(
        num_scalar_prefetch=1, grid=(DB, n_pages // pg),
        in_specs=[page_spec(p) for p in range(pg)]
        + [per_seq((H_IDX * ds, D_IDX)), per_seq((ds, H_IDX)), per_seq((ds, D_IDX))],
        out_specs=[per_group((n_pages + 1, ds, PAGE_SIZE)), per_group((ds, PAGE_SIZE))],
        scratch_shapes=[pltpu.VMEM((H_IDX, ds, PAGE_SIZE), F32)])
    return pl.pallas_call(
        functools.partial(_idx_sample_kernel, pg=pg, n_pages=n_pages, ds=ds, gb=gb, topk=topk,
                          idx_bits=int(L).bit_length()),
        grid_spec=grid_spec,
        out_shape=(jax.ShapeDtypeStruct((DB, n_pages + 1, ds, PAGE_SIZE), F32),
                   jax.ShapeDtypeStruct((DB, ds, PAGE_SIZE), F32)),
        compiler_params=_cparams(2), name="idx_sample",
    )(page_table_flat, *([kidx_pages] * pg), qi_st, w_s, ki_new_b)


def _attn_sample_kernel(pt_ref, *refs, pg, n_pages, ds):
    k_refs, v_refs = refs[:pg], refs[pg:2 * pg]
    q_ref, kn_ref, vn_ref, sc_ref, thr_ref, out_ref, qbd_ref, acc_ref, m_ref, l_ref = refs[2 * pg:]
    s_id = pl.program_id(1)
    n_steps = n_pages // pg
    hq = N_HEADS * ds
    head_of_row = lax.broadcasted_iota(I32, (hq, ATTN_DIM), 0) // ds
    head_of_col = lax.broadcasted_iota(I32, (hq, ATTN_DIM), 1) // HEAD_DIM

    @pl.when(s_id == 0)
    def _():
        q_rows = jnp.concatenate([q_ref[0]] * N_HEADS, axis=0)
        qbd_ref[...] = jnp.where(head_of_row == head_of_col, q_rows, jnp.zeros_like(q_rows))
        m_ref[...] = jnp.full((hq, 1), NEG_BIG, F32)
        l_ref[...] = jnp.zeros((hq, 1), F32)
        acc_ref[...] = jnp.zeros((hq, ATTN_DIM), F32)

    nn, nt = (((1,), (0,)), ((), ())), (((1,), (1,)), ((), ()))

    def attend(kbs, vbs, sels, keys_on_lanes):
        ss = [lax.dot_general(qbd_ref[...], kb, nn if keys_on_lanes else nt, preferred_element_type=F32)
              + jnp.concatenate([jnp.where(sel, 0.0, -jnp.inf)] * N_HEADS, axis=0)
              for kb, sel in zip(kbs, sels)]
        blk_max = functools.reduce(jnp.maximum, ss)
        m_old = m_ref[...]
        m_new = jnp.maximum(m_old, jnp.max(blk_max, axis=1, keepdims=True))
        alpha = jnp.exp2(m_old - m_new)
        ps = [jnp.exp2(s - m_new) for s in ss]
        l_ref[...] = alpha * l_ref[...] + jnp.sum(functools.reduce(jnp.add, ps), axis=1, keepdims=True)
        m_ref[...] = m_new
        pv = functools.reduce(jnp.add, [
            lax.dot_general(p.astype(BF16), vb, nt if keys_on_lanes else nn, preferred_element_type=F32)
            for p, vb in zip(ps, vbs)])
        acc_ref[...] = alpha * acc_ref[...] + pv

    thr = thr_ref[0]
    attend([r[0].astype(BF16) for r in k_refs], [r[0].astype(BF16) for r in v_refs],
           [sc_ref[0, s_id * pg + p] >= thr for p in range(pg)], True)

    @pl.when(s_id == n_steps - 1)
    def _():
        pad = jnp.zeros((PAGE_SIZE - ds, ATTN_DIM), BF16)
        lane = lax.broadcasted_iota(I32, (ds, PAGE_SIZE), 1)
        trow = lax.broadcasted_iota(I32, (ds, PAGE_SIZE), 0)
        sel = (sc_ref[0, n_pages] >= thr) & (lane <= trow)
        attend([jnp.concatenate([kn_ref[0], pad], axis=0)], [jnp.concatenate([vn_ref[0], pad], axis=0)],
               [sel], False)
        o = jnp.where(head_of_row == head_of_col, acc_ref[...] / l_ref[...], 0.0)
        out = o[0:ds]
        for h in range(1, N_HEADS):
            out = out + o[h * ds:(h + 1) * ds]
        out_ref[0] = out.astype(BF16)


def _attn_sample(page_table_flat, k_pages, v_pages, q_s, k_new_b, v_new_b, scores, thr, *, pg, n_pages, ds):
    DB = q_s.shape[0]
    page_spec = lambda p: pl.BlockSpec((1, ATTN_DIM, PAGE_SIZE),
                                       lambda b, s, pt: (pt[b * n_pages + s * pg + p], 0, 0))
    per_seq = lambda shp: pl.BlockSpec((1,) + shp, lambda b, s, pt: (b,) + (0,) * len(shp))
    hq = N_HEADS * ds
    grid_spec = pltpu.PrefetchScalarGridSpec(
        num_scalar_prefetch=1, grid=(DB, n_pages // pg),
        in_specs=[page_spec(p) for p in range(pg)] * 2
        + [per_seq((ds, ATTN_DIM))] * 3 + [per_seq((n_pages + 1, ds, PAGE_SIZE)), per_seq((ds, PAGE_SIZE))],
        out_specs=per_seq((ds, ATTN_DIM)),
        scratch_shapes=[pltpu.VMEM((hq, ATTN_DIM), BF16), pltpu.VMEM((hq, ATTN_DIM), F32),
                        pltpu.VMEM((hq, 1), F32), pltpu.VMEM((hq, 1), F32)])
    return pl.pallas_call(
        functools.partial(_attn_sample_kernel, pg=pg, n_pages=n_pages, ds=ds),
        grid_spec=grid_spec, out_shape=jax.ShapeDtypeStruct((DB, ds, ATTN_DIM), BF16),
        compiler_params=_cparams(2), name="attn_sample",
    )(page_table_flat, *([k_pages] * pg), *([v_pages] * pg), q_s, k_new_b, v_new_b, scores, thr)


def _rope_tables(pos):
    half = HEAD_DIM // 2
    inv = ROPE_THETA ** (-jnp.arange(half, dtype=F32) / half)
    ang = pos.astype(F32)[:, None] * inv[None, :]
    cos, sin = jnp.cos(ang), jnp.sin(ang)
    return jnp.tile(cos, (1, 4)), jnp.tile(jnp.concatenate([-sin, sin], axis=1), (1, 2))


def _pick(n, prefs):
    for p in prefs:
        if n % p == 0:
            return p
    raise ValueError(f"no supported tile for extent {n}")


def kernel(x_prompt, x_sample, cache_k, cache_v, cache_kidx, state_conv, page_table, norm_attn_pre,
           norm_attn_post, w_in, w_dw, b_dw, conv_ln_g, conv_ln_b, w_out, norm_mlp_pre, norm_mlp_post,
           w_up, w_down):
    B, S, _ = x_prompt.shape
    DB, DS, _ = x_sample.shape
    n_pages = page_table.shape[1]
    past_len = n_pages * PAGE_SIZE
    assert w_in.shape[0] == 1, "single-layer kernel"
    assert DS == SUBLANES, "sample rows per sequence must fill one sublane tile"

    w_in_b = jnp.pad(w_in[0], ((0, 0), (0, D_IN_PAD - D_IN))).astype(BF16)
    w_out_b, w_up_b, w_dn_b = w_out[0].astype(BF16), w_up[0].astype(BF16), w_down[0].astype(BF16)
    g_pre, g_post = norm_attn_pre, norm_attn_post
    g_mpre, g_mpost = norm_mlp_pre, norm_mlp_post

    tm = _pick(S, (256, 128))
    tk = _pick(S, (256, 128))
    tq = _pick(S, (256, 128))
    cos_p, sin_p = _rope_tables(jnp.arange(S, dtype=I32))
    assert tm == tq, "the front kernel lays qi out per attention query tile"
    (kT_p, vT_p, kiT_p, k4, vT4, qT, qi2, wT, ki4, kn2, conv_p, state_p) = _front_prompt(
        x_prompt, g_pre, w_in_b, cos_p, sin_p, w_dw[0], b_dw, conv_ln_g, conv_ln_b, tm=tm, tk=tk)
    attn_p = _attn_prompt(qT, qi2, wT, k4, vT4, ki4, kn2, tq=tq, tk=tk)
    y_p = _back(x_prompt.reshape(B * S, D_MODEL), conv_p.reshape(B * S, C_CONV),
                attn_p.reshape(B * S, ATTN_DIM), w_out_b, g_post, g_mpre, g_mpost, w_up_b, w_dn_b,
                tm=tm).reshape(B, S, D_MODEL)

    R = DB * DS
    gs = _pick(DB, (32, 16, 8, 4, 2, 1))
    cos_s, sin_s = _rope_tables(past_len + jnp.arange(DS, dtype=I32))
    cos_s, sin_s = jnp.tile(cos_s, (gs, 1)), jnp.tile(sin_s, (gs, 1))
    (k_s, v_s, ki_s, q_s, qi_s, w_s, conv_s, state_s) = _front_sample(
        x_sample.reshape(R, D_MODEL), g_pre, w_in_b, cos_s, sin_s, w_dw[0], b_dw, conv_ln_g, conv_ln_b,
        state_conv[0], gs=gs, ds=DS)
    pg = _pick(n_pages, (16, 8, 4, 2, 1))
    pt_flat = page_table.reshape(-1)
    qi_st = qi_s.reshape(DB, DS, H_IDX, D_IDX).transpose(0, 2, 1, 3).reshape(DB, H_IDX * DS, D_IDX)
    w_sq = w_s[:, D_IDX:D_IDX + H_IDX].reshape(DB, DS, H_IDX)
    n_phys = cache_k.shape[1]
    pagesT = lambda c: jnp.transpose(c[0], (0, 2, 3, 1)).reshape(n_phys, ATTN_DIM, PAGE_SIZE)
    scores, thr = _idx_sample(pt_flat, jnp.transpose(cache_kidx[0], (0, 2, 1)), qi_st, w_sq,
                              ki_s.astype(BF16).reshape(DB, DS, D_IDX),
                              pg=_pick(n_pages, (64, 32, 16, 8, 4, 2, 1)), n_pages=n_pages, ds=DS)
    attn_s = _attn_sample(pt_flat, pagesT(cache_k), pagesT(cache_v),
                          q_s.reshape(DB, DS, ATTN_DIM), k_s.astype(BF16).reshape(DB, DS, ATTN_DIM),
                          v_s.astype(BF16).reshape(DB, DS, ATTN_DIM), scores, thr,
                          pg=pg, n_pages=n_pages, ds=DS)
    y_s = _back(x_sample.reshape(R, D_MODEL), conv_s, attn_s.reshape(R, ATTN_DIM), w_out_b, g_post, g_mpre,
                g_mpost, w_up_b, w_dn_b, tm=_pick(R, (256, 128, 64, 32, 16, 8))).reshape(DB, DS, D_MODEL)

    hd = (N_HEADS, HEAD_DIM)
    unT = lambda t: jnp.transpose(t.reshape(B, N_HEADS, HEAD_DIM, S), (0, 3, 1, 2))[None]
    return (y_p, y_s,
            unT(kT_p), unT(vT_p), jnp.transpose(kiT_p, (0, 2, 1))[None], state_p[None],
            k_s.reshape(1, DB, DS, *hd), v_s.reshape(1, DB, DS, *hd), ki_s.reshape(1, DB, DS, D_IDX),
            state_s[None])
```

```python
import functools

import jax
import jax.numpy as jnp
import numpy as np
from jax import lax
from jax.experimental import pallas as pl
from jax.experimental.pallas import tpu as pltpu

F32 = jnp.float32
BF16 = jnp.bfloat16
I32 = jnp.int32

D_MODEL = 1024
C_CONV = 512
CONV_WIDTH = 31
CONV_HALO = CONV_WIDTH - 1
N_HEADS = 8
HEAD_DIM = 64
ATTN_DIM = N_HEADS * HEAD_DIM
H_IDX = 8
D_IDX = 64
TOPK_MAX = 256
D_FF = 4 * D_MODEL
ROPE_THETA = 10000.0
EPS = 1e-6
PAGE_SIZE = 128

COL_A, COL_G, COL_Q, COL_K, COL_V, COL_QI, COL_KW = 0, 512, 1024, 1536, 2048, 2560, 3072
D_IN = 3144
LANES = 128
SUBLANES = 8
D_IN_PAD = COL_KW + LANES

INT_MIN = -(2 ** 31)
F32_INF_BITS = 0x7F800000
NEG_BIG = -1e30
BF16_ROWS = 16
V_ROWS = HEAD_DIM + BF16_ROWS
LOG2_E = 1.4426950408889634
MAX_RAW_LOGIT = 64.0
NORM_SLACK = 1.05

VMEM_LIMIT = 56 * 1024 * 1024


def _cparams(n_axes):
    return pltpu.CompilerParams(dimension_semantics=("arbitrary",) * n_axes,
                                vmem_limit_bytes=VMEM_LIMIT)


def _rms(x):
    return x * lax.rsqrt(jnp.mean(x * x, axis=-1, keepdims=True) + EPS)


def _rope_slab(x, cos, sin_signed, first_half):
    outs = []
    for c in range(x.shape[1] // LANES):
        xc = x[:, c * LANES:(c + 1) * LANES]
        partner = jnp.where(first_half, pltpu.roll(xc, LANES - 32, 1), pltpu.roll(xc, 32, 1))
        outs.append(xc * cos + partner * sin_signed)
    return outs[0] if len(outs) == 1 else jnp.concatenate(outs, axis=1)


def _projector(x, g, w_ref, cos, sin_signed):
    hb = (_rms(x) * g).astype(BF16)
    dot = lambda c0, n: jnp.dot(hb, w_ref[:, c0:c0 + n], preferred_element_type=F32)
    first_half = (lax.broadcasted_iota(I32, (x.shape[0], LANES), 1) % 64) < 32
    rope = lambda z: _rope_slab(z, cos, sin_signed, first_half)

    def kw():
        z = dot(COL_KW, LANES)
        return rope(z), z * ((H_IDX ** -0.5) * (D_IDX ** -0.5))

    return dict(a=lambda: dot(COL_A, C_CONV), gate=lambda: dot(COL_G, C_CONV),
                q=lambda: rope(dot(COL_Q, ATTN_DIM)) * (HEAD_DIM ** -0.5 * LOG2_E),
                k=lambda: rope(dot(COL_K, ATTN_DIM)), v=lambda: dot(COL_V, ATTN_DIM),
                qi=lambda: rope(dot(COL_QI, H_IDX * D_IDX)), kw=kw)


def _project(x, g, w_ref, cos, sin_signed):
    p = _projector(x, g, w_ref, cos, sin_signed)
    a, gate, q, k, v, qi = (p[n]() for n in ("a", "gate", "q", "k", "v", "qi"))
    return (a, gate, q, k, v, qi) + p["kw"]()


def _glu(a, gate):
    return a * (1.0 / (1.0 + jnp.exp(-gate)))


def _ln_swish(y, ln_g, ln_b):
    mu = jnp.mean(y, axis=-1, keepdims=True)
    d = y - mu
    var = jnp.mean(d * d, axis=-1, keepdims=True)
    y = (d * lax.rsqrt(var + EPS)) * ln_g + ln_b
    return y * (1.0 / (1.0 + jnp.exp(-y)))


def _key_to_f32(c):
    c = jnp.maximum(c, -F32_INF_BITS)
    bits = jnp.where(c >= 0, c, INT_MIN - c)
    return lax.bitcast_convert_type(bits, F32)


def _f32_to_key(x):
    bits = lax.bitcast_convert_type(x, I32)
    return jnp.where(bits >= 0, bits, INT_MIN - bits)


def _bisect_threshold(count_ge, lo, hi, topk):
    def still_open(lo, hi, cnt):
        return (cnt != topk) & (hi > lo + 1)

    def step(lo, hi, cnt):
        act = still_open(lo, hi, cnt)
        mid = jnp.right_shift(lo, 1) + jnp.right_shift(hi, 1) + (lo & hi & 1)
        c = count_ge(_key_to_f32(mid))
        take = act & (c >= topk)
        drop = act & (c < topk)
        return jnp.where(take, mid, lo), jnp.where(drop, mid, hi), jnp.where(take, c, cnt)

    def body(st):
        lo, hi, cnt, _ = st
        lo, hi, cnt = step(*step(lo, hi, cnt))
        return lo, hi, cnt, jnp.max(jnp.where(still_open(lo, hi, cnt), 1.0, 0.0)) > 0.0

    init = (lo, hi, jnp.full(lo.shape, jnp.inf, F32), jnp.bool_(True))
    lo, _, cnt, _ = lax.while_loop(lambda st: st[3], body, init)
    return _key_to_f32(lo), cnt


def _bisect_tie_index(count_eq_below, need, shape, nbits):
    def body(b, c):
        cand = c + lax.shift_left(jnp.int32(1), jnp.int32(nbits - 1) - b)
        return jnp.where(count_eq_below(cand) < need, cand, c)
    return lax.fori_loop(0, nbits, body, jnp.zeros(shape, I32))


def _front_prompt_kernel(x_ref, g_ref, w_ref, cos_ref, sin_ref, wdw_ref, bdw_ref, lng_ref, lnb_ref,
                         kT_out, vT_out, kiT_out, kb_out, vTb_out, qT_out, qi2_out, wT_out, kib_out, kn2_out,
                         conv_out, state_out, ext_ref, sh_ref, *, tm, tk, conv_rows):
    j = pl.program_id(1)
    proj = _projector(x_ref[0], g_ref[...], w_ref, cos_ref[...], sin_ref[...])

    cur = lax.bitwise_and(j, 1)
    prev = 1 - cur

    @pl.when(j == 0)
    def _():
        ext_ref[1] = jnp.zeros((tm + 32, C_CONV), F32)
    ext_prev = ext_ref.at[prev]
    for r in range(1, SUBLANES):
        sh_ref[r - 1] = ext_prev[r:r + tm + 24, :]

    def conv_rows_from(r0):
        acc = jnp.zeros((conv_rows, C_CONV), F32)
        for t in range(CONV_WIDTH):
            lo = r0 + 32 - CONV_HALO + t
            r = lo % SUBLANES
            src = ext_prev if r == 0 else sh_ref.at[r - 1]
            acc = acc + wdw_ref[t:t + 1, :] * src[lo - r:lo - r + conv_rows, :]
        y = _ln_swish(acc + bdw_ref[...], lng_ref[...], lnb_ref[...])
        conv_out[0, r0:r0 + conv_rows, :] = y.astype(BF16)
    conv_chunks = iter(range(0, tm, conv_rows))

    def conv_step():
        r0 = next(conv_chunks, None)
        if r0 is not None:
            conv_rows_from(r0)

    k = proj["k"]()
    kT = k.T
    kT_out[0] = kT
    kn2 = jnp.max(jnp.sum((kT * kT).reshape(N_HEADS, HEAD_DIM, tm), axis=1), axis=1, keepdims=True)
    kn2_out[0, 0] = jnp.broadcast_to(kn2, (N_HEADS, LANES))
    kb = k.astype(BF16)
    for c in range(tm // tk):
        kb_out[0, c] = kb[c * tk:(c + 1) * tk]
    conv_step()

    vT = proj["v"]().T
    vT_out[0] = vT
    vTb = vT.astype(BF16)
    ones = jnp.ones((V_ROWS - HEAD_DIM, tm), BF16)
    vTx = jnp.concatenate([blk for h in range(N_HEADS)
                           for blk in (vTb[h * HEAD_DIM:(h + 1) * HEAD_DIM], ones)], axis=0)
    for c in range(tm // tk):
        vTb_out[0, c] = vTx[:, c * tk:(c + 1) * tk]
    conv_step()

    qT_out[0] = proj["q"]().T.astype(BF16)
    conv_step()

    qiT = proj["qi"]().T.astype(BF16)
    qi2_out[0, 0] = jnp.concatenate([qiT[h * D_IDX:(h + 1) * D_IDX] for h in range(H_IDX)], axis=1)
    conv_step()

    ki_slab, w_eff = proj["kw"]()
    kiT_out[0] = ki_slab.T[:D_IDX]
    kib = ki_slab[:, :D_IDX].astype(BF16)
    for c in range(tm // tk):
        kib_out[0, c] = kib[c * tk:(c + 1) * tk]
    wT_out[0] = w_eff.T[D_IDX:D_IDX + H_IDX]
    for _ in conv_chunks:
        conv_rows_from(_)

    state_out[0] = ext_prev[32 + tm - CONV_HALO:32 + tm, :]
    ext_ref[cur, 0:32, :] = ext_prev[tm:tm + 32, :]
    ext_ref[cur, 32:32 + tm, :] = _glu(proj["a"](), proj["gate"]())


def _front_prompt(x, g_pre, w_in_b, cos, sin_signed, w_dw, b_dw, ln_g, ln_b, *, tm, tk):
    B, S, _ = x.shape
    n_c = S // tk
    nj = S // tm
    t = lambda j: jnp.minimum(j, nj - 1)
    row = lambda w: pl.BlockSpec((1, tm, w), lambda b, j: (b, t(j), 0))
    const = lambda shp: pl.BlockSpec(shp, lambda b, j: (0,) * len(shp))
    chunked = lambda d2, d3: pl.BlockSpec((1, tm // tk, d2, d3), lambda b, j: (b, t(j), 0, 0))
    colT = lambda r: pl.BlockSpec((1, r, tm), lambda b, j: (b, 0, t(j)))
    out_shape = (
        jax.ShapeDtypeStruct((B, ATTN_DIM, S), F32),
        jax.ShapeDtypeStruct((B, ATTN_DIM, S), F32),
        jax.ShapeDtypeStruct((B, D_IDX, S), F32),
        jax.ShapeDtypeStruct((B, n_c, tk, ATTN_DIM), BF16),
        jax.ShapeDtypeStruct((B, n_c, N_HEADS * V_ROWS, tk), BF16),
        jax.ShapeDtypeStruct((B, ATTN_DIM, S), BF16),
        jax.ShapeDtypeStruct((B, S // tm, D_IDX, H_IDX * tm), BF16),
        jax.ShapeDtypeStruct((B, H_IDX, S), F32),
        jax.ShapeDtypeStruct((B, n_c, tk, D_IDX), BF16),
        jax.ShapeDtypeStruct((B, S // tm, N_HEADS, LANES), F32),
        jax.ShapeDtypeStruct((B, S, C_CONV), BF16),
        jax.ShapeDtypeStruct((B, CONV_HALO, C_CONV), F32),
    )
    out_specs = (colT(ATTN_DIM), colT(ATTN_DIM), colT(D_IDX), chunked(tk, ATTN_DIM),
                 chunked(N_HEADS * V_ROWS, tk), colT(ATTN_DIM),
                 pl.BlockSpec((1, 1, D_IDX, H_IDX * tm), lambda b, j: (b, t(j), 0, 0)),
                 colT(H_IDX), chunked(tk, D_IDX),
                 pl.BlockSpec((1, 1, N_HEADS, LANES), lambda b, j: (b, t(j), 0, 0)),
                 pl.BlockSpec((1, tm, C_CONV), lambda b, j: (b, jnp.maximum(j - 1, 0), 0)),
                 pl.BlockSpec((1, CONV_HALO, C_CONV), lambda b, j: (b, 0, 0)))
    in_specs = [row(D_MODEL), const((1, D_MODEL)), const((D_MODEL, D_IN_PAD)),
                pl.BlockSpec((tm, LANES), lambda b, j: (t(j), 0)), pl.BlockSpec((tm, LANES), lambda b, j: (t(j), 0)),
                const((CONV_WIDTH, C_CONV)), const((1, C_CONV)), const((1, C_CONV)), const((1, C_CONV))]
    return pl.pallas_call(
        functools.partial(_front_prompt_kernel, tm=tm, tk=tk, conv_rows=64),
        grid=(B, nj + 1), in_specs=in_specs, out_specs=out_specs, out_shape=out_shape,
        scratch_shapes=[pltpu.VMEM((2, tm + 32, C_CONV), F32),
                        pltpu.VMEM((SUBLANES - 1, tm + 24, C_CONV), F32)],
        compiler_params=_cparams(2), name="front_prompt",
    )(x, g_pre, w_in_b, cos, sin_signed, w_dw, b_dw, ln_g, ln_b)


def _attn_prompt_kernel(qT_ref, qi2_ref, wT_ref, k_ref, vT_ref, ki_ref, kn2_ref, out_ref,
                        sc_ref, gm_ref, qz_ref, acc_ref, m_ref, *, tq, tk, topk, idx_bits):
    i = pl.program_id(1)
    q0 = i * tq
    n_diag = tq // tk
    n_full = i * n_diag
    n_chunks = n_full + n_diag
    n_rg = tk // SUBLANES
    row_iota = lax.broadcasted_iota(I32, (tk, tq), 0)
    qpos = q0 + lax.broadcasted_iota(I32, (tk, tq), 1)

    def over_chunks(stage1, stage2):
        def pair(g, carry):
            a0, a1 = stage1(2 * g, False), stage1(2 * g + 1, False)
            stage2(2 * g, a0, False)
            stage2(2 * g + 1, a1, False)
            return carry
        lax.fori_loop(0, lax.shift_right_logical(n_full, 1), pair, 0)

        @pl.when(lax.bitwise_and(n_full, 1) == 1)
        def _():
            stage2(n_full - 1, stage1(n_full - 1, False), False)
        for dch in range(n_diag):
            stage2(n_full + dch, stage1(n_full + dch, True), True)

    def score_dots(c, causal):
        return jnp.dot(ki_ref[0, c], qi2_ref[0, 0], preferred_element_type=F32)

    def score_reduce(c, d, causal):
        s = jnp.zeros((tk, tq), F32)
        for h in range(H_IDX):
            s = s + wT_ref[0, h:h + 1, :] * jnp.maximum(d[:, h * tq:(h + 1) * tq], 0.0)
        if causal:
            s = jnp.where(c * tk + row_iota <= qpos, s, -jnp.inf)
        sc_ref[c] = s
        gm_ref[...] = jnp.maximum(gm_ref[...], s)

    gm_ref[...] = jnp.full((tk, tq), -jnp.inf, F32)
    over_chunks(score_dots, score_reduce)

    @pl.when(lax.bitwise_and(n_chunks, 1) == 1)
    def _():
        sc_ref[n_chunks] = jnp.full((tk, tq), -jnp.inf, F32)

    def count(pred):
        def body(g, acc):
            for u in range(2):
                c = 2 * g + u
                m = jnp.where(pred(c, sc_ref[c]), 1.0, 0.0).reshape(n_rg, SUBLANES, tq)
                acc = acc + jnp.sum(m, axis=0)
            return acc
        acc = lax.fori_loop(0, lax.shift_right_logical(n_chunks + 1, 1), body, jnp.zeros((SUBLANES, tq), F32))
        return jnp.sum(acc, axis=0, keepdims=True)

    gm = gm_ref[...]
    hi_key = _f32_to_key(jnp.max(gm, axis=0, keepdims=True)) + 1
    lo_val = jnp.min(gm, axis=0, keepdims=True) if tk >= topk else jnp.full((1, tq), -jnp.inf, F32)
    thr, n_ge = _bisect_threshold(lambda t: count(lambda c, s: s >= t), _f32_to_key(lo_val), hi_key, topk)
    unknown = (n_ge == jnp.inf) & (thr > -jnp.inf)
    n_ge = lax.cond(jnp.max(jnp.where(unknown, 1.0, 0.0)) > 0.0,
                    lambda: jnp.where(unknown, count(lambda c, s: s >= thr), n_ge), lambda: n_ge)
    tied = (n_ge > topk) & (thr > -jnp.inf)

    @pl.when(jnp.max(jnp.where(tied, 1.0, 0.0)) > 0.0)
    def _():
        need = topk - count(lambda c, s: s > thr)
        last = _bisect_tie_index(
            lambda cand: count(lambda c, s: (s == thr) & (c * tk + row_iota < cand)),
            need, (1, tq), idx_bits)

        def drop(c, carry):
            s = sc_ref[c]
            sc_ref[c] = jnp.where(tied & (s == thr) & (c * tk + row_iota > last), -jnp.inf, s)
            return carry
        lax.fori_loop(0, n_chunks, drop, 0)

    zeros_half = jnp.zeros((HEAD_DIM, tq), BF16)
    for pr in range(N_HEADS // 2):
        q_even = qT_ref[0, (2 * pr) * HEAD_DIM:(2 * pr + 1) * HEAD_DIM, :]
        q_odd = qT_ref[0, (2 * pr + 1) * HEAD_DIM:(2 * pr + 2) * HEAD_DIM, :]
        qz_ref[pr] = jnp.concatenate([jnp.concatenate([q_even, zeros_half], axis=1),
                                      jnp.concatenate([zeros_half, q_odd], axis=1)], axis=0)
    m_ref[...] = jnp.full((N_HEADS, SUBLANES, tq), NEG_BIG, F32)
    acc_ref[...] = jnp.zeros((N_HEADS * V_ROWS, tq), F32)

    qf = qT_ref[0].astype(F32)
    qn2 = jnp.sum((qf * qf).reshape(N_HEADS, HEAD_DIM, tq), axis=1)
    kn2 = jnp.max(kn2_ref[0], axis=0)[:, 0:1]
    small_logits = jnp.max(qn2 * kn2) * NORM_SLACK <= MAX_RAW_LOGIT ** 2

    def masked_bias(c, causal):
        sel = sc_ref[c] >= thr
        if causal:
            sel = sel & (c * tk + row_iota <= qpos)
        return jnp.where(sel, 0.0, -jnp.inf)

    def raw_probs(c, causal):
        neg = masked_bias(c, causal)
        probs = []
        for pr in range(N_HEADS // 2):
            d = jnp.dot(k_ref[0, c, :, pr * LANES:(pr + 1) * LANES], qz_ref[pr], preferred_element_type=F32)
            probs += [jnp.exp2(d[:, e * tq:(e + 1) * tq] + neg).astype(BF16) for e in range(2)]
        return probs

    def raw_accumulate(c, probs, causal):
        for h in range(N_HEADS):
            rows = slice(h * V_ROWS, (h + 1) * V_ROWS)
            acc_ref[rows, :] += jnp.dot(vT_ref[0, c, rows, :], probs[h], preferred_element_type=F32)

    def attend_chunk(c, _, causal):
        neg = masked_bias(c, causal)
        masked, col_max = [], []
        for pr in range(N_HEADS // 2):
            d = jnp.dot(k_ref[0, c, :, pr * LANES:(pr + 1) * LANES], qz_ref[pr], preferred_element_type=F32)
            for e in range(2):
                s = (d[:, e * tq:(e + 1) * tq] + neg).reshape(n_rg, SUBLANES, tq)
                masked.append(s)
                col_max.append(jnp.max(jnp.max(s, axis=0), axis=0, keepdims=True))
        for h in range(N_HEADS):
            s = masked[h]
            m_old = m_ref[h]
            m_new = jnp.maximum(m_old, col_max[h])
            alpha = jnp.exp2(m_old - m_new)
            p = jnp.exp2(s - m_new[None]).reshape(tk, tq).astype(BF16)
            m_ref[h] = m_new
            rows = slice(h * V_ROWS, (h + 1) * V_ROWS)
            pv = jnp.dot(vT_ref[0, c, rows, :], p, preferred_element_type=F32)
            acc = acc_ref[rows, :].reshape(V_ROWS // SUBLANES, SUBLANES, tq) * alpha[None]
            acc_ref[rows, :] = acc.reshape(V_ROWS, tq) + pv

    @pl.when(small_logits)
    def _():
        over_chunks(raw_probs, raw_accumulate)

    @pl.when(jnp.logical_not(small_logits))
    def _():
        over_chunks(lambda c, causal: None, attend_chunk)

    outs = []
    for h in range(N_HEADS):
        denom = acc_ref[h * V_ROWS + HEAD_DIM:h * V_ROWS + HEAD_DIM + 1, :]
        outs.append(acc_ref[h * V_ROWS:h * V_ROWS + HEAD_DIM, :] / denom)
    out_ref[0] = jnp.concatenate(outs, axis=0).T.astype(BF16)


def _attn_prompt(qT, qi2, wT, k4, vT4, ki4, kn2, *, tq, tk):
    B, n_c = k4.shape[:2]
    S = n_c * tk
    topk = min(TOPK_MAX, S // 4)
    whole = lambda shp: pl.BlockSpec((1,) + shp, lambda b, i: (b,) + (0,) * len(shp))
    colT = lambda r: pl.BlockSpec((1, r, tq), lambda b, i: (b, 0, i))
    return pl.pallas_call(
        functools.partial(_attn_prompt_kernel, tq=tq, tk=tk, topk=topk, idx_bits=int(S).bit_length()),
        grid=(B, S // tq),
        in_specs=[colT(ATTN_DIM), pl.BlockSpec((1, 1, D_IDX, H_IDX * tq), lambda b, i: (b, i, 0, 0)), colT(H_IDX),
                  whole((n_c, tk, ATTN_DIM)), whole((n_c, N_HEADS * V_ROWS, tk)), whole((n_c, tk, D_IDX)),
                  whole(kn2.shape[1:])],
        out_specs=pl.BlockSpec((1, tq, ATTN_DIM), lambda b, i: (b, i, 0)),
        out_shape=jax.ShapeDtypeStruct((B, S, ATTN_DIM), BF16),
        scratch_shapes=[pltpu.VMEM((n_c + 1, tk, tq), F32),
                        pltpu.VMEM((tk, tq), F32),
                        pltpu.VMEM((N_HEADS // 2, LANES, 2 * tq), BF16),
                        pltpu.VMEM((N_HEADS * V_ROWS, tq), F32),
                        pltpu.VMEM((N_HEADS, SUBLANES, tq), F32)],
        compiler_params=_cparams(2), name="attn_prompt",
    )(qT, qi2, wT, k4, vT4, ki4, kn2)


def _back_kernel(x_ref, c_ref, a_ref, wo_ref, gpost_ref, gmpre_ref, gmpost_ref, wup_ref, wdn_ref, y_ref,
                 *, ff_chunk):
    mix = (jnp.dot(c_ref[...], wo_ref[0:C_CONV, :], preferred_element_type=F32)
           + jnp.dot(a_ref[...], wo_ref[C_CONV:, :], preferred_element_type=F32))
    x1 = x_ref[...] + _rms(mix) * gpost_ref[...]
    hm = (_rms(x1) * gmpre_ref[...]).astype(BF16)
    m = jnp.zeros_like(x1)
    for f in range(0, D_FF, ff_chunk):
        up = jnp.dot(hm, wup_ref[:, f:f + ff_chunk], preferred_element_type=F32)
        act = jnp.square(jnp.maximum(up, 0.0)).astype(BF16)
        m = m + jnp.dot(act, wdn_ref[f:f + ff_chunk, :], preferred_element_type=F32)
    y_ref[...] = x1 + _rms(m) * gmpost_ref[...]


def _back(x2, conv2, attn2, w_out_b, g_post, g_mpre, g_mpost, w_up_b, w_dn_b, *, tm):
    R = x2.shape[0]
    row = lambda w: pl.BlockSpec((tm, w), lambda r: (r, 0))
    const = lambda shp: pl.BlockSpec(shp, lambda r: (0,) * len(shp))
    return pl.pallas_call(
        functools.partial(_back_kernel, ff_chunk=1024),
        grid=(R // tm,),
        in_specs=[row(D_MODEL), row(C_CONV), row(ATTN_DIM), const((D_MODEL, D_MODEL)),
                  const((1, D_MODEL)), const((1, D_MODEL)), const((1, D_MODEL)),
                  const((D_MODEL, D_FF)), const((D_FF, D_MODEL))],
        out_specs=row(D_MODEL), out_shape=jax.ShapeDtypeStruct((R, D_MODEL), F32),
        compiler_params=_cparams(1), name="back",
    )(x2, conv2, attn2, w_out_b, g_post, g_mpre, g_mpost, w_up_b, w_dn_b)


def _front_sample_kernel(x_ref, g_ref, w_ref, cos_ref, sin_ref, wdw_ref, bdw_ref, lng_ref, lnb_ref, st_ref,
                         k_out, v_out, ki_out, q_out, qi_out, w_out, conv_out, state_out, ext_ref, *, gs, ds):
    a, gate, q, k, v, qi, ki_slab, w_eff = _project(x_ref[...], g_ref[...], w_ref, cos_ref[...], sin_ref[...])
    k_out[...] = k
    v_out[...] = v
    ki_out[...] = ki_slab[:, :D_IDX]
    q_out[...] = q.astype(BF16)
    qi_out[...] = qi.astype(BF16)
    w_out[...] = w_eff

    ext_ref[:, 0:CONV_HALO, :] = st_ref[...]
    ext_ref[:, CONV_HALO:CONV_HALO + ds, :] = _glu(a, gate).reshape(gs, ds, C_CONV)
    acc = jnp.zeros((gs, ds, C_CONV), F32)
    for t in range(CONV_WIDTH):
        acc = acc + wdw_ref[t:t + 1, :] * ext_ref[:, t:t + ds, :]
    y = _ln_swish(acc.reshape(gs * ds, C_CONV) + bdw_ref[...], lng_ref[...], lnb_ref[...])
    conv_out[...] = y.astype(BF16)
    state_out[...] = ext_ref[:, ds:ds + CONV_HALO, :]


def _front_sample(x2, g_pre, w_in_b, cos, sin_signed, w_dw, b_dw, ln_g, ln_b, state, *, gs, ds):
    R = x2.shape[0]
    DB = R // ds
    tm = gs * ds
    row = lambda w: pl.BlockSpec((tm, w), lambda r: (r, 0))
    const = lambda shp: pl.BlockSpec(shp, lambda r: (0,) * len(shp))
    st_spec = pl.BlockSpec((gs, CONV_HALO, C_CONV), lambda r: (r, 0, 0))
    out_shape = (
        jax.ShapeDtypeStruct((R, ATTN_DIM), F32), jax.ShapeDtypeStruct((R, ATTN_DIM), F32),
        jax.ShapeDtypeStruct((R, D_IDX), F32),
        jax.ShapeDtypeStruct((R, ATTN_DIM), BF16), jax.ShapeDtypeStruct((R, H_IDX * D_IDX), BF16),
        jax.ShapeDtypeStruct((R, LANES), F32),
        jax.ShapeDtypeStruct((R, C_CONV), BF16), jax.ShapeDtypeStruct((DB, CONV_HALO, C_CONV), F32))
    out_specs = (row(ATTN_DIM), row(ATTN_DIM), row(D_IDX), row(ATTN_DIM), row(H_IDX * D_IDX), row(LANES),
                 row(C_CONV), st_spec)
    return pl.pallas_call(
        functools.partial(_front_sample_kernel, gs=gs, ds=ds),
        grid=(R // tm,),
        in_specs=[row(D_MODEL), const((1, D_MODEL)), const((D_MODEL, D_IN_PAD)),
                  const((tm, LANES)), const((tm, LANES)),
                  const((CONV_WIDTH, C_CONV)), const((1, C_CONV)), const((1, C_CONV)), const((1, C_CONV)),
                  st_spec],
        out_specs=out_specs, out_shape=out_shape,
        scratch_shapes=[pltpu.VMEM((gs, CONV_HALO + ds + 2, C_CONV), F32)],
        compiler_params=_cparams(1), name="front_sample",
    )(x2, g_pre, w_in_b, cos, sin_signed, w_dw, b_dw, ln_g, ln_b, state)


def _idx_sample_kernel(pt_ref, *refs, pg, n_pages, ds, gb, topk, idx_bits):
    page_refs = refs[:pg]
    qi_ref, w_ref, kin_ref, sc_ref, thr_ref, wb_ref = refs[pg:]
    s_id = pl.program_id(1)
    bb = pl.program_id(0) % gb
    n_steps = n_pages // pg
    lane = lax.broadcasted_iota(I32, (ds, PAGE_SIZE), 1)
    trow = lax.broadcasted_iota(I32, (ds, PAGE_SIZE), 0)

    @pl.when(s_id == 0)
    def _():
        for h in range(H_IDX):
            wb_ref[h] = jnp.broadcast_to(w_ref[0, :, h:h + 1], (ds, PAGE_SIZE))

    def scores(keys_b, keys_on_lanes):
        contract = (((1,), (0,)), ((), ())) if keys_on_lanes else (((1,), (1,)), ((), ()))
        d = lax.dot_general(qi_ref[0], keys_b, contract, preferred_element_type=F32)
        s = jnp.zeros((ds, PAGE_SIZE), F32)
        for h in range(H_IDX):
            s = s + wb_ref[h] * jnp.maximum(d[h * ds:(h + 1) * ds], 0.0)
        return s

    for p in range(pg):
        sc_ref[bb, s_id * pg + p] = scores(page_refs[p][0].astype(BF16), True)

    @pl.when(s_id == n_steps - 1)
    def _():
        kin = jnp.concatenate([kin_ref[0], jnp.zeros((PAGE_SIZE - ds, D_IDX), BF16)], axis=0)
        sc_ref[bb, n_pages] = jnp.where(lane <= trow, scores(kin, False), -jnp.inf)

    @pl.when((s_id == n_steps - 1) & (bb == gb - 1))
    def _():
        shape4 = (gb, n_pages + 1, ds, PAGE_SIZE)

        def count(pred):
            hits = jnp.where(pred(sc_ref[...]), 1.0, 0.0)
            return jnp.sum(jnp.sum(hits, axis=1, keepdims=True), axis=3, keepdims=True)

        shape = (gb, 1, ds, 1)
        thr, n_ge = _bisect_threshold(lambda t: count(lambda s: s >= t), jnp.full(shape, -F32_INF_BITS, I32),
                                      jnp.full(shape, F32_INF_BITS, I32), topk)
        tied = (n_ge > topk) & (thr > -jnp.inf)

        @pl.when(jnp.max(jnp.where(tied, 1.0, 0.0)) > 0.0)
        def _():
            key_idx = (lax.broadcasted_iota(I32, shape4, 1) * PAGE_SIZE
                       + lax.broadcasted_iota(I32, shape4, 3))
            need = topk - count(lambda s: s > thr)
            last = _bisect_tie_index(lambda cand: count(lambda s: (s == thr) & (key_idx < cand)),
                                     need, (gb, 1, ds, 1), idx_bits)
            s = sc_ref[...]
            sc_ref[...] = jnp.where(tied & (s == thr) & (key_idx > last), -jnp.inf, s)

        thr_ref[...] = jnp.broadcast_to(thr.reshape(gb, ds, 1), (gb, ds, PAGE_SIZE))


def _idx_sample(page_table_flat, kidx_pages, qi_st, w_s, ki_new_b, *, pg, n_pages, ds):
    DB = qi_st.shape[0]
    gb = _pick(DB, (16, 8, 4, 2, 1))
    L = n_pages * PAGE_SIZE + ds
    topk = min(TOPK_MAX, L // 4)
    page_spec = lambda p: pl.BlockSpec((1, D_IDX, PAGE_SIZE),
                                       lambda b, s, pt: (pt[b * n_pages + s * pg + p], 0, 0))
    per_seq = lambda shp: pl.BlockSpec((1,) + shp, lambda b, s, pt: (b,) + (0,) * len(shp))
    per_group = lambda shp: pl.BlockSpec((gb,) + shp, lambda b, s, pt: (b // gb,) + (0,) * len(shp))
    grid_spec = pltpu.PrefetchScalarGridSpec(
        num_scalar_prefetch=1, grid=(DB, n_pages // pg),
        in_specs=[page_spec(p) for p in range(pg)]
        + [per_seq((H_IDX * ds, D_IDX)), per_seq((ds, H_IDX)), per_seq((ds, D_IDX))],
        out_specs=[per_group((n_pages + 1, ds, PAGE_SIZE)), per_group((ds, PAGE_SIZE))],
        scratch_shapes=[pltpu.VMEM((H_IDX, ds, PAGE_SIZE), F32)])
    return pl.pallas_call(
        functools.partial(_idx_sample_kernel, pg=pg, n_pages=n_pages, ds=ds, gb=gb, topk=topk,
                          idx_bits=int(L).bit_length()),
        grid_spec=grid_spec,
        out_shape=(jax.ShapeDtypeStruct((DB, n_pages + 1, ds, PAGE_SIZE), F32),
                   jax.ShapeDtypeStruct((DB, ds, PAGE_SIZE), F32)),
        compiler_params=_cparams(2), name="idx_sample",
    )(page_table_flat, *([kidx_pages] * pg), qi_st, w_s, ki_new_b)


def _attn_sample_kernel(pt_ref, *refs, pg, n_pages, ds):
    k_refs, v_refs = refs[:pg], refs[pg:2 * pg]
    q_ref, kn_ref, vn_ref, sc_ref, thr_ref, out_ref, qbd_ref, acc_ref, m_ref, l_ref = refs[2 * pg:]
    s_id = pl.program_id(1)
    n_steps = n_pages // pg
    hq = N_HEADS * ds
    head_of_row = lax.broadcasted_iota(I32, (hq, ATTN_DIM), 0) // ds
    head_of_col = lax.broadcasted_iota(I32, (hq, ATTN_DIM), 1) // HEAD_DIM

    @pl.when(s_id == 0)
    def _():
        q_rows = jnp.concatenate([q_ref[0]] * N_HEADS, axis=0)
        qbd_ref[...] = jnp.where(head_of_row == head_of_col, q_rows, jnp.zeros_like(q_rows))
        m_ref[...] = jnp.full((hq, 1), NEG_BIG, F32)
        l_ref[...] = jnp.zeros((hq, 1), F32)
        acc_ref[...] = jnp.zeros((hq, ATTN_DIM), F32)

    nn, nt = (((1,), (0,)), ((), ())), (((1,), (1,)), ((), ()))

    def attend(kbs, vbs, sels, keys_on_lanes):
        ss = [lax.dot_general(qbd_ref[...], kb, nn if keys_on_lanes else nt, preferred_element_type=F32)
              + jnp.concatenate([jnp.where(sel, 0.0, -jnp.inf)] * N_HEADS, axis=0)
              for kb, sel in zip(kbs, sels)]
        blk_max = functools.reduce(jnp.maximum, ss)
        m_old = m_ref[...]
        m_new = jnp.maximum(m_old, jnp.max(blk_max, axis=1, keepdims=True))
        alpha = jnp.exp2(m_old - m_new)
        ps = [jnp.exp2(s - m_new) for s in ss]
        l_ref[...] = alpha * l_ref[...] + jnp.sum(functools.reduce(jnp.add, ps), axis=1, keepdims=True)
        m_ref[...] = m_new
        pv = functools.reduce(jnp.add, [
            lax.dot_general(p.astype(BF16), vb, nt if keys_on_lanes else nn, preferred_element_type=F32)
            for p, vb in zip(ps, vbs)])
        acc_ref[...] = alpha * acc_ref[...] + pv

    thr = thr_ref[0]
    attend([r[0].astype(BF16) for r in k_refs], [r[0].astype(BF16) for r in v_refs],
           [sc_ref[0, s_id * pg + p] >= thr for p in range(pg)], True)

    @pl.when(s_id == n_steps - 1)
    def _():
        pad = jnp.zeros((PAGE_SIZE - ds, ATTN_DIM), BF16)
        lane = lax.broadcasted_iota(I32, (ds, PAGE_SIZE), 1)
        trow = lax.broadcasted_iota(I32, (ds, PAGE_SIZE), 0)
        sel = (sc_ref[0, n_pages] >= thr) & (lane <= trow)
        attend([jnp.concatenate([kn_ref[0], pad], axis=0)], [jnp.concatenate([vn_ref[0], pad], axis=0)],
               [sel], False)
        o = jnp.where(head_of_row == head_of_col, acc_ref[...] / l_ref[...], 0.0)
        out = o[0:ds]
        for h in range(1, N_HEADS):
            out = out + o[h * ds:(h + 1) * ds]
        out_ref[0] = out.astype(BF16)


def _attn_sample(page_table_flat, k_pages, v_pages, q_s, k_new_b, v_new_b, scores, thr, *, pg, n_pages, ds):
    DB = q_s.shape[0]
    page_spec = lambda p: pl.BlockSpec((1, ATTN_DIM, PAGE_SIZE),
                                       lambda b, s, pt: (pt[b * n_pages + s * pg + p], 0, 0))
    per_seq = lambda shp: pl.BlockSpec((1,) + shp, lambda b, s, pt: (b,) + (0,) * len(shp))
    hq = N_HEADS * ds
    grid_spec = pltpu.PrefetchScalarGridSpec(
        num_scalar_prefetch=1, grid=(DB, n_pages // pg),
        in_specs=[page_spec(p) for p in range(pg)] * 2
        + [per_seq((ds, ATTN_DIM))] * 3 + [per_seq((n_pages + 1, ds, PAGE_SIZE)), per_seq((ds, PAGE_SIZE))],
        out_specs=per_seq((ds, ATTN_DIM)),
        scratch_shapes=[pltpu.VMEM((hq, ATTN_DIM), BF16), pltpu.VMEM((hq, ATTN_DIM), F32),
                        pltpu.VMEM((hq, 1), F32), pltpu.VMEM((hq, 1), F32)])
    return pl.pallas_call(
        functools.partial(_attn_sample_kernel, pg=pg, n_pages=n_pages, ds=ds),
        grid_spec=grid_spec, out_shape=jax.ShapeDtypeStruct((DB, ds, ATTN_DIM), BF16),
        compiler_params=_cparams(2), name="attn_sample",
    )(page_table_flat, *([k_pages] * pg), *([v_pages] * pg), q_s, k_new_b, v_new_b, scores, thr)


def _rope_tables(pos):
    half = HEAD_DIM // 2
    inv = ROPE_THETA ** (-jnp.arange(half, dtype=F32) / half)
    ang = pos.astype(F32)[:, None] * inv[None, :]
    cos, sin = jnp.cos(ang), jnp.sin(ang)
    return jnp.tile(cos, (1, 4)), jnp.tile(jnp.concatenate([-sin, sin], axis=1), (1, 2))


def _pick(n, prefs):
    for p in prefs:
        if n % p == 0:
            return p
    raise ValueError(f"no supported tile for extent {n}")


def kernel(x_prompt, x_sample, cache_k, cache_v, cache_kidx, state_conv, page_table, norm_attn_pre,
           norm_attn_post, w_in, w_dw, b_dw, conv_ln_g, conv_ln_b, w_out, norm_mlp_pre, norm_mlp_post,
           w_up, w_down):
    B, S, _ = x_prompt.shape
    DB, DS, _ = x_sample.shape
    n_pages = page_table.shape[1]
    past_len = n_pages * PAGE_SIZE
    assert w_in.shape[0] == 1, "single-layer kernel"
    assert DS == SUBLANES, "sample rows per sequence must fill one sublane tile"

    w_in_b = jnp.pad(w_in[0], ((0, 0), (0, D_IN_PAD - D_IN))).astype(BF16)
    w_out_b, w_up_b, w_dn_b = w_out[0].astype(BF16), w_up[0].astype(BF16), w_down[0].astype(BF16)
    g_pre, g_post = norm_attn_pre, norm_attn_post
    g_mpre, g_mpost = norm_mlp_pre, norm_mlp_post

    tm = _pick(S, (256, 128))
    tk = _pick(S, (256, 128))
    tq = _pick(S, (256, 128))
    cos_p, sin_p = _rope_tables(jnp.arange(S, dtype=I32))
    assert tm == tq, "the front kernel lays qi out per attention query tile"
    (kT_p, vT_p, kiT_p, k4, vT4, qT, qi2, wT, ki4, kn2, conv_p, state_p) = _front_prompt(
        x_prompt, g_pre, w_in_b, cos_p, sin_p, w_dw[0], b_dw, conv_ln_g, conv_ln_b, tm=tm, tk=tk)
    attn_p = _attn_prompt(qT, qi2, wT, k4, vT4, ki4, kn2, tq=tq, tk=tk)
    y_p = _back(x_prompt.reshape(B * S, D_MODEL), conv_p.reshape(B * S, C_CONV),
                attn_p.reshape(B * S, ATTN_DIM), w_out_b, g_post, g_mpre, g_mpost, w_up_b, w_dn_b,
                tm=tm).reshape(B, S, D_MODEL)

    R = DB * DS
    gs = _pick(DB, (32, 16, 8, 4, 2, 1))
    cos_s, sin_s = _rope_tables(past_len + jnp.arange(DS, dtype=I32))
    cos_s, sin_s = jnp.tile(cos_s, (gs, 1)), jnp.tile(sin_s, (gs, 1))
    (k_s, v_s, ki_s, q_s, qi_s, w_s, conv_s, state_s) = _front_sample(
        x_sample.reshape(R, D_MODEL), g_pre, w_in_b, cos_s, sin_s, w_dw[0], b_dw, conv_ln_g, conv_ln_b,
        state_conv[0], gs=gs, ds=DS)
    pg = _pick(n_pages, (32, 16, 8, 4, 2, 1))
    pt_flat = page_table.reshape(-1)
    qi_st = qi_s.reshape(DB, DS, H_IDX, D_IDX).transpose(0, 2, 1, 3).reshape(DB, H_IDX * DS, D_IDX)
    w_sq = w_s[:, D_IDX:D_IDX + H_IDX].reshape(DB, DS, H_IDX)
    n_phys = cache_k.shape[1]
    pagesT = lambda c: jnp.transpose(c[0], (0, 2, 3, 1)).reshape(n_phys, ATTN_DIM, PAGE_SIZE)
    scores, thr = _idx_sample(pt_flat, jnp.transpose(cache_kidx[0], (0, 2, 1)), qi_st, w_sq,
                              ki_s.astype(BF16).reshape(DB, DS, D_IDX),
                              pg=_pick(n_pages, (64, 32, 16, 8, 4, 2, 1)), n_pages=n_pages, ds=DS)
    attn_s = _attn_sample(pt_flat, pagesT(cache_k), pagesT(cache_v),
                          q_s.reshape(DB, DS, ATTN_DIM), k_s.astype(BF16).reshape(DB, DS, ATTN_DIM),
                          v_s.astype(BF16).reshape(DB, DS, ATTN_DIM), scores, thr,
                          pg=pg, n_pages=n_pages, ds=DS)
    y_s = _back(x_sample.reshape(R, D_MODEL), conv_s, attn_s.reshape(R, ATTN_DIM), w_out_b, g_post, g_mpre,
                g_mpost, w_up_b, w_dn_b, tm=_pick(R, (256, 128, 64, 32, 16, 8))).reshape(DB, DS, D_MODEL)

    hd = (N_HEADS, HEAD_DIM)
    unT = lambda t: jnp.transpose(t.reshape(B, N_HEADS, HEAD_DIM, S), (0, 3, 1, 2))[None]
    return (y_p, y_s,
            unT(kT_p), unT(vT_p), jnp.transpose(kiT_p, (0, 2, 1))[None], state_p[None],
            k_s.reshape(1, DB, DS, *hd), v_s.reshape(1, DB, DS, *hd), ki_s.reshape(1, DB, DS, D_IDX),
            state_s[None])
```

```python
import functools

import jax
import jax.numpy as jnp
import numpy as np
from jax import lax
from jax.experimental import pallas as pl
from jax.experimental.pallas import tpu as pltpu

F32 = jnp.float32
BF16 = jnp.bfloat16
I32 = jnp.int32

D_MODEL = 1024
C_CONV = 512
CONV_WIDTH = 31
CONV_HALO = CONV_WIDTH - 1
N_HEADS = 8
HEAD_DIM = 64
ATTN_DIM = N_HEADS * HEAD_DIM
H_IDX = 8
D_IDX = 64
TOPK_MAX = 256
D_FF = 4 * D_MODEL
ROPE_THETA = 10000.0
EPS = 1e-6
PAGE_SIZE = 128

COL_A, COL_G, COL_Q, COL_K, COL_V, COL_QI, COL_KW = 0, 512, 1024, 1536, 2048, 2560, 3072
D_IN = 3144
LANES = 128
SUBLANES = 8
D_IN_PAD = COL_KW + LANES

INT_MIN = -(2 ** 31)
F32_INF_BITS = 0x7F800000
NEG_BIG = -1e30
BF16_ROWS = 16
V_ROWS = HEAD_DIM + BF16_ROWS
LOG2_E = 1.4426950408889634
MAX_RAW_LOGIT = 64.0
NORM_SLACK = 1.05

VMEM_LIMIT = 56 * 1024 * 1024


def _cparams(n_axes):
    return pltpu.CompilerParams(dimension_semantics=("arbitrary",) * n_axes,
                                vmem_limit_bytes=VMEM_LIMIT)


def _rms(x):
    return x * lax.rsqrt(jnp.mean(x * x, axis=-1, keepdims=True) + EPS)


def _rope_slab(x, cos, sin_signed, first_half):
    outs = []
    for c in range(x.shape[1] // LANES):
        xc = x[:, c * LANES:(c + 1) * LANES]
        partner = jnp.where(first_half, pltpu.roll(xc, LANES - 32, 1), pltpu.roll(xc, 32, 1))
        outs.append(xc * cos + partner * sin_signed)
    return outs[0] if len(outs) == 1 else jnp.concatenate(outs, axis=1)


def _projector(x, g, w_ref, cos, sin_signed):
    hb = (_rms(x) * g).astype(BF16)
    dot = lambda c0, n: jnp.dot(hb, w_ref[:, c0:c0 + n], preferred_element_type=F32)
    first_half = (lax.broadcasted_iota(I32, (x.shape[0], LANES), 1) % 64) < 32
    rope = lambda z: _rope_slab(z, cos, sin_signed, first_half)

    def kw():
        z = dot(COL_KW, LANES)
        return rope(z), z * ((H_IDX ** -0.5) * (D_IDX ** -0.5))

    return dict(a=lambda: dot(COL_A, C_CONV), gate=lambda: dot(COL_G, C_CONV),
                q=lambda: rope(dot(COL_Q, ATTN_DIM)) * (HEAD_DIM ** -0.5 * LOG2_E),
                k=lambda: rope(dot(COL_K, ATTN_DIM)), v=lambda: dot(COL_V, ATTN_DIM),
                qi=lambda: rope(dot(COL_QI, H_IDX * D_IDX)), kw=kw)


def _project(x, g, w_ref, cos, sin_signed):
    p = _projector(x, g, w_ref, cos, sin_signed)
    a, gate, q, k, v, qi = (p[n]() for n in ("a", "gate", "q", "k", "v", "qi"))
    return (a, gate, q, k, v, qi) + p["kw"]()


def _glu(a, gate):
    return a * (1.0 / (1.0 + jnp.exp(-gate)))


def _ln_swish(y, ln_g, ln_b):
    mu = jnp.mean(y, axis=-1, keepdims=True)
    d = y - mu
    var = jnp.mean(d * d, axis=-1, keepdims=True)
    y = (d * lax.rsqrt(var + EPS)) * ln_g + ln_b
    return y * (1.0 / (1.0 + jnp.exp(-y)))


def _key_to_f32(c):
    c = jnp.maximum(c, -F32_INF_BITS)
    bits = jnp.where(c >= 0, c, INT_MIN - c)
    return lax.bitcast_convert_type(bits, F32)


def _f32_to_key(x):
    bits = lax.bitcast_convert_type(x, I32)
    return jnp.where(bits >= 0, bits, INT_MIN - bits)


def _bisect_threshold(count_ge, lo, hi, topk, unchecked_steps=14):
    def still_open(lo, hi, cnt):
        return (cnt != topk) & (hi > lo + 1)

    def step(lo, hi, cnt):
        act = still_open(lo, hi, cnt)
        mid = jnp.right_shift(lo, 1) + jnp.right_shift(hi, 1) + (lo & hi & 1)
        c = count_ge(_key_to_f32(mid))
        take = act & (c >= topk)
        drop = act & (c < topk)
        return jnp.where(take, mid, lo), jnp.where(drop, mid, hi), jnp.where(take, c, cnt)

    def body(st):
        lo, hi, cnt, _ = st
        lo, hi, cnt = step(*step(lo, hi, cnt))
        return lo, hi, cnt, jnp.max(jnp.where(still_open(lo, hi, cnt), 1.0, 0.0)) > 0.0

    st = lax.fori_loop(0, unchecked_steps // 2, lambda _, st: step(*step(*st)),
                       (lo, hi, jnp.full(lo.shape, jnp.inf, F32)))
    lo, _, cnt, _ = lax.while_loop(lambda st: st[3], body, st + (jnp.bool_(True),))
    return _key_to_f32(lo), cnt


def _bisect_tie_index(count_eq_below, need, shape, nbits):
    def body(b, c):
        cand = c + lax.shift_left(jnp.int32(1), jnp.int32(nbits - 1) - b)
        return jnp.where(count_eq_below(cand) < need, cand, c)
    return lax.fori_loop(0, nbits, body, jnp.zeros(shape, I32))


def _front_prompt_kernel(x_ref, g_ref, w_ref, cos_ref, sin_ref, wdw_ref, bdw_ref, lng_ref, lnb_ref,
                         kT_out, vT_out, kiT_out, kb_out, vTb_out, qT_out, qi2_out, wT_out, kib_out, kn2_out,
                         conv_out, state_out, ext_ref, sh_ref, *, tm, tk, conv_rows):
    j = pl.program_id(1)
    proj = _projector(x_ref[0], g_ref[...], w_ref, cos_ref[...], sin_ref[...])

    cur = lax.bitwise_and(j, 1)
    prev = 1 - cur

    @pl.when(j == 0)
    def _():
        ext_ref[1] = jnp.zeros((tm + 32, C_CONV), F32)
    ext_prev = ext_ref.at[prev]
    for r in range(1, SUBLANES):
        sh_ref[r - 1] = ext_prev[r:r + tm + 24, :]

    def conv_rows_from(r0):
        acc = jnp.zeros((conv_rows, C_CONV), F32)
        for t in range(CONV_WIDTH):
            lo = r0 + 32 - CONV_HALO + t
            r = lo % SUBLANES
            src = ext_prev if r == 0 else sh_ref.at[r - 1]
            acc = acc + wdw_ref[t:t + 1, :] * src[lo - r:lo - r + conv_rows, :]
        y = _ln_swish(acc + bdw_ref[...], lng_ref[...], lnb_ref[...])
        conv_out[0, r0:r0 + conv_rows, :] = y.astype(BF16)
    conv_chunks = iter(range(0, tm, conv_rows))

    def conv_step():
        r0 = next(conv_chunks, None)
        if r0 is not None:
            conv_rows_from(r0)

    k = proj["k"]()
    kT = k.T
    kT_out[0] = kT
    kn2 = jnp.max(jnp.sum((kT * kT).reshape(N_HEADS, HEAD_DIM, tm), axis=1), axis=1, keepdims=True)
    kn2_out[0, 0] = jnp.broadcast_to(kn2, (N_HEADS, LANES))
    kb = k.astype(BF16)
    for c in range(tm // tk):
        kb_out[0, c] = kb[c * tk:(c + 1) * tk]
    conv_step()

    vT = proj["v"]().T
    vT_out[0] = vT
    vTb = vT.astype(BF16)
    ones = jnp.ones((V_ROWS - HEAD_DIM, tm), BF16)
    vTx = jnp.concatenate([blk for h in range(N_HEADS)
                           for blk in (vTb[h * HEAD_DIM:(h + 1) * HEAD_DIM], ones)], axis=0)
    for c in range(tm // tk):
        vTb_out[0, c] = vTx[:, c * tk:(c + 1) * tk]
    conv_step()

    qT_out[0] = proj["q"]().T.astype(BF16)
    conv_step()

    qiT = proj["qi"]().T.astype(BF16)
    qi2_out[0, 0] = jnp.concatenate([qiT[h * D_IDX:(h + 1) * D_IDX] for h in range(H_IDX)], axis=1)
    conv_step()

    ki_slab, w_eff = proj["kw"]()
    kiT_out[0] = ki_slab.T[:D_IDX]
    kib = ki_slab[:, :D_IDX].astype(BF16)
    for c in range(tm // tk):
        kib_out[0, c] = kib[c * tk:(c + 1) * tk]
    wT_out[0] = w_eff.T[D_IDX:D_IDX + H_IDX]
    for _ in conv_chunks:
        conv_rows_from(_)

    state_out[0] = ext_prev[32 + tm - CONV_HALO:32 + tm, :]
    ext_ref[cur, 0:32, :] = ext_prev[tm:tm + 32, :]
    ext_ref[cur, 32:32 + tm, :] = _glu(proj["a"](), proj["gate"]())


def _front_prompt(x, g_pre, w_in_b, cos, sin_signed, w_dw, b_dw, ln_g, ln_b, *, tm, tk):
    B, S, _ = x.shape
    n_c = S // tk
    nj = S // tm
    t = lambda j: jnp.minimum(j, nj - 1)
    row = lambda w: pl.BlockSpec((1, tm, w), lambda b, j: (b, t(j), 0))
    const = lambda shp: pl.BlockSpec(shp, lambda b, j: (0,) * len(shp))
    chunked = lambda d2, d3: pl.BlockSpec((1, tm // tk, d2, d3), lambda b, j: (b, t(j), 0, 0))
    colT = lambda r: pl.BlockSpec((1, r, tm), lambda b, j: (b, 0, t(j)))
    out_shape = (
        jax.ShapeDtypeStruct((B, ATTN_DIM, S), F32),
        jax.ShapeDtypeStruct((B, ATTN_DIM, S), F32),
        jax.ShapeDtypeStruct((B, D_IDX, S), F32),
        jax.ShapeDtypeStruct((B, n_c, tk, ATTN_DIM), BF16),
        jax.ShapeDtypeStruct((B, n_c, N_HEADS * V_ROWS, tk), BF16),
        jax.ShapeDtypeStruct((B, ATTN_DIM, S), BF16),
        jax.ShapeDtypeStruct((B, S // tm, D_IDX, H_IDX * tm), BF16),
        jax.ShapeDtypeStruct((B, H_IDX, S), F32),
        jax.ShapeDtypeStruct((B, n_c, tk, D_IDX), BF16),
        jax.ShapeDtypeStruct((B, S // tm, N_HEADS, LANES), F32),
        jax.ShapeDtypeStruct((B, S, C_CONV), BF16),
        jax.ShapeDtypeStruct((B, CONV_HALO, C_CONV), F32),
    )
    out_specs = (colT(ATTN_DIM), colT(ATTN_DIM), colT(D_IDX), chunked(tk, ATTN_DIM),
                 chunked(N_HEADS * V_ROWS, tk), colT(ATTN_DIM),
                 pl.BlockSpec((1, 1, D_IDX, H_IDX * tm), lambda b, j: (b, t(j), 0, 0)),
                 colT(H_IDX), chunked(tk, D_IDX),
                 pl.BlockSpec((1, 1, N_HEADS, LANES), lambda b, j: (b, t(j), 0, 0)),
                 pl.BlockSpec((1, tm, C_CONV), lambda b, j: (b, jnp.maximum(j - 1, 0), 0)),
                 pl.BlockSpec((1, CONV_HALO, C_CONV), lambda b, j: (b, 0, 0)))
    in_specs = [row(D_MODEL), const((1, D_MODEL)), const((D_MODEL, D_IN_PAD)),
                pl.BlockSpec((tm, LANES), lambda b, j: (t(j), 0)), pl.BlockSpec((tm, LANES), lambda b, j: (t(j), 0)),
                const((CONV_WIDTH, C_CONV)), const((1, C_CONV)), const((1, C_CONV)), const((1, C_CONV))]
    return pl.pallas_call(
        functools.partial(_front_prompt_kernel, tm=tm, tk=tk, conv_rows=64),
        grid=(B, nj + 1), in_specs=in_specs, out_specs=out_specs, out_shape=out_shape,
        scratch_shapes=[pltpu.VMEM((2, tm + 32, C_CONV), F32),
                        pltpu.VMEM((SUBLANES - 1, tm + 24, C_CONV), F32)],
        compiler_params=_cparams(2), name="front_prompt",
    )(x, g_pre, w_in_b, cos, sin_signed, w_dw, b_dw, ln_g, ln_b)


def _attn_prompt_kernel(qT_ref, qi2_ref, wT_ref, k_ref, vT_ref, ki_ref, kn2_ref, out_ref,
                        sc_ref, gm_ref, qz_ref, acc_ref, m_ref, *, tq, tk, topk, idx_bits):
    i = pl.program_id(1)
    q0 = i * tq
    n_diag = tq // tk
    n_full = i * n_diag
    n_chunks = n_full + n_diag
    n_rg = tk // SUBLANES
    row_iota = lax.broadcasted_iota(I32, (tk, tq), 0)
    qpos = q0 + lax.broadcasted_iota(I32, (tk, tq), 1)

    qf = qT_ref[0].astype(F32)
    qn2 = jnp.sum((qf * qf).reshape(N_HEADS, HEAD_DIM, tq), axis=1)
    kn2 = jnp.max(kn2_ref[0], axis=0)[:, 0:1]
    small_logits = jnp.max(qn2 * kn2) * NORM_SLACK <= MAX_RAW_LOGIT ** 2

    def over_chunks(stage1, stage2):
        def pair(g, carry):
            a0, a1 = stage1(2 * g, False), stage1(2 * g + 1, False)
            stage2(2 * g, a0, False)
            stage2(2 * g + 1, a1, False)
            return carry
        lax.fori_loop(0, lax.shift_right_logical(n_full, 1), pair, 0)

        @pl.when(lax.bitwise_and(n_full, 1) == 1)
        def _():
            stage2(n_full - 1, stage1(n_full - 1, False), False)
        for dch in range(n_diag):
            stage2(n_full + dch, stage1(n_full + dch, True), True)

    def score_dots(c, causal):
        return jnp.dot(ki_ref[0, c], qi2_ref[0, 0], preferred_element_type=F32)

    def score_reduce(c, d, causal):
        s = jnp.zeros((tk, tq), F32)
        for h in range(H_IDX):
            s = s + wT_ref[0, h:h + 1, :] * jnp.maximum(d[:, h * tq:(h + 1) * tq], 0.0)
        if causal:
            s = jnp.where(c * tk + row_iota <= qpos, s, -jnp.inf)
        sc_ref[c] = s
        gm_ref[...] = jnp.maximum(gm_ref[...], s)

    gm_ref[...] = jnp.full((tk, tq), -jnp.inf, F32)
    over_chunks(score_dots, score_reduce)

    @pl.when(lax.bitwise_and(n_chunks, 1) == 1)
    def _():
        sc_ref[n_chunks] = jnp.full((tk, tq), -jnp.inf, F32)

    def count(pred):
        def body(g, acc):
            for u in range(2):
                c = 2 * g + u
                m = jnp.where(pred(c, sc_ref[c]), 1.0, 0.0).reshape(n_rg, SUBLANES, tq)
                acc = acc + jnp.sum(m, axis=0)
            return acc
        acc = lax.fori_loop(0, lax.shift_right_logical(n_chunks + 1, 1), body, jnp.zeros((SUBLANES, tq), F32))
        return jnp.sum(acc, axis=0, keepdims=True)

    gm = gm_ref[...]
    hi_key = _f32_to_key(jnp.max(gm, axis=0, keepdims=True)) + 1
    lo_val = jnp.min(gm, axis=0, keepdims=True) if tk >= topk else jnp.full((1, tq), -jnp.inf, F32)
    thr, n_ge = _bisect_threshold(lambda t: count(lambda c, s: s >= t), _f32_to_key(lo_val), hi_key, topk)
    unknown = (n_ge == jnp.inf) & (thr > -jnp.inf)
    n_ge = lax.cond(jnp.max(jnp.where(unknown, 1.0, 0.0)) > 0.0,
                    lambda: jnp.where(unknown, count(lambda c, s: s >= thr), n_ge), lambda: n_ge)
    tied = (n_ge > topk) & (thr > -jnp.inf)

    @pl.when(jnp.max(jnp.where(tied, 1.0, 0.0)) > 0.0)
    def _():
        need = topk - count(lambda c, s: s > thr)
        last = _bisect_tie_index(
            lambda cand: count(lambda c, s: (s == thr) & (c * tk + row_iota < cand)),
            need, (1, tq), idx_bits)

        def drop(c, carry):
            s = sc_ref[c]
            sc_ref[c] = jnp.where(tied & (s == thr) & (c * tk + row_iota > last), -jnp.inf, s)
            return carry
        lax.fori_loop(0, n_chunks, drop, 0)

    zeros_half = jnp.zeros((HEAD_DIM, tq), BF16)
    for pr in range(N_HEADS // 2):
        q_even = qT_ref[0, (2 * pr) * HEAD_DIM:(2 * pr + 1) * HEAD_DIM, :]
        q_odd = qT_ref[0, (2 * pr + 1) * HEAD_DIM:(2 * pr + 2) * HEAD_DIM, :]
        qz_ref[pr] = jnp.concatenate([jnp.concatenate([q_even, zeros_half], axis=1),
                                      jnp.concatenate([zeros_half, q_odd], axis=1)], axis=0)
    m_ref[...] = jnp.full((N_HEADS, SUBLANES, tq), NEG_BIG, F32)
    acc_ref[...] = jnp.zeros((N_HEADS * V_ROWS, tq), F32)

    def masked_bias(c, causal):
        sel = sc_ref[c] >= thr
        if causal:
            sel = sel & (c * tk + row_iota <= qpos)
        return jnp.where(sel, 0.0, -jnp.inf)

    def raw_probs(c, causal):
        neg = masked_bias(c, causal)
        probs = []
        for pr in range(N_HEADS // 2):
            d = jnp.dot(k_ref[0, c, :, pr * LANES:(pr + 1) * LANES], qz_ref[pr], preferred_element_type=F32)
            probs += [jnp.exp2(d[:, e * tq:(e + 1) * tq] + neg).astype(BF16) for e in range(2)]
        return probs

    def raw_accumulate(c, probs, causal):
        for h in range(N_HEADS):
            rows = slice(h * V_ROWS, (h + 1) * V_ROWS)
            acc_ref[rows, :] += jnp.dot(vT_ref[0, c, rows, :], probs[h], preferred_element_type=F32)

    def attend_chunk(c, _, causal):
        neg = masked_bias(c, causal)
        masked, col_max = [], []
        for pr in range(N_HEADS // 2):
            d = jnp.dot(k_ref[0, c, :, pr * LANES:(pr + 1) * LANES], qz_ref[pr], preferred_element_type=F32)
            for e in range(2):
                s = (d[:, e * tq:(e + 1) * tq] + neg).reshape(n_rg, SUBLANES, tq)
                masked.append(s)
                col_max.append(jnp.max(jnp.max(s, axis=0), axis=0, keepdims=True))
        for h in range(N_HEADS):
            s = masked[h]
            m_old = m_ref[h]
            m_new = jnp.maximum(m_old, col_max[h])
            alpha = jnp.exp2(m_old - m_new)
            p = jnp.exp2(s - m_new[None]).reshape(tk, tq).astype(BF16)
            m_ref[h] = m_new
            rows = slice(h * V_ROWS, (h + 1) * V_ROWS)
            pv = jnp.dot(vT_ref[0, c, rows, :], p, preferred_element_type=F32)
            acc = acc_ref[rows, :].reshape(V_ROWS // SUBLANES, SUBLANES, tq) * alpha[None]
            acc_ref[rows, :] = acc.reshape(V_ROWS, tq) + pv

    @pl.when(small_logits)
    def _():
        over_chunks(raw_probs, raw_accumulate)

    @pl.when(jnp.logical_not(small_logits))
    def _():
        over_chunks(lambda c, causal: None, attend_chunk)

    outs = []
    for h in range(N_HEADS):
        denom = acc_ref[h * V_ROWS + HEAD_DIM:h * V_ROWS + HEAD_DIM + 1, :]
        outs.append(acc_ref[h * V_ROWS:h * V_ROWS + HEAD_DIM, :] / denom)
    out_ref[0] = jnp.concatenate(outs, axis=0).T.astype(BF16)


def _attn_prompt(qT, qi2, wT, k4, vT4, ki4, kn2, *, tq, tk):
    B, n_c = k4.shape[:2]
    S = n_c * tk
    topk = min(TOPK_MAX, S // 4)
    whole = lambda shp: pl.BlockSpec((1,) + shp, lambda b, i: (b,) + (0,) * len(shp))
    colT = lambda r: pl.BlockSpec((1, r, tq), lambda b, i: (b, 0, i))
    return pl.pallas_call(
        functools.partial(_attn_prompt_kernel, tq=tq, tk=tk, topk=topk, idx_bits=int(S).bit_length()),
        grid=(B, S // tq),
        in_specs=[colT(ATTN_DIM), pl.BlockSpec((1, 1, D_IDX, H_IDX * tq), lambda b, i: (b, i, 0, 0)), colT(H_IDX),
                  whole((n_c, tk, ATTN_DIM)), whole((n_c, N_HEADS * V_ROWS, tk)), whole((n_c, tk, D_IDX)),
                  whole(kn2.shape[1:])],
        out_specs=pl.BlockSpec((1, tq, ATTN_DIM), lambda b, i: (b, i, 0)),
        out_shape=jax.ShapeDtypeStruct((B, S, ATTN_DIM), BF16),
        scratch_shapes=[pltpu.VMEM((n_c + 1, tk, tq), F32),
                        pltpu.VMEM((tk, tq), F32),
                        pltpu.VMEM((N_HEADS // 2, LANES, 2 * tq), BF16),
                        pltpu.VMEM((N_HEADS * V_ROWS, tq), F32),
                        pltpu.VMEM((N_HEADS, SUBLANES, tq), F32)],
        compiler_params=_cparams(2), name="attn_prompt",
    )(qT, qi2, wT, k4, vT4, ki4, kn2)


def _back_kernel(x_ref, c_ref, a_ref, wo_ref, gpost_ref, gmpre_ref, gmpost_ref, wup_ref, wdn_ref, y_ref,
                 *, ff_chunk):
    mix = (jnp.dot(c_ref[...], wo_ref[0:C_CONV, :], preferred_element_type=F32)
           + jnp.dot(a_ref[...], wo_ref[C_CONV:, :], preferred_element_type=F32))
    x1 = x_ref[...] + _rms(mix) * gpost_ref[...]
    hm = (_rms(x1) * gmpre_ref[...]).astype(BF16)
    m = jnp.zeros_like(x1)
    for f in range(0, D_FF, ff_chunk):
        up = jnp.dot(hm, wup_ref[:, f:f + ff_chunk], preferred_element_type=F32)
        act = jnp.square(jnp.maximum(up, 0.0)).astype(BF16)
        m = m + jnp.dot(act, wdn_ref[f:f + ff_chunk, :], preferred_element_type=F32)
    y_ref[...] = x1 + _rms(m) * gmpost_ref[...]


def _back(x2, conv2, attn2, w_out_b, g_post, g_mpre, g_mpost, w_up_b, w_dn_b, *, tm):
    R = x2.shape[0]
    row = lambda w: pl.BlockSpec((tm, w), lambda r: (r, 0))
    const = lambda shp: pl.BlockSpec(shp, lambda r: (0,) * len(shp))
    return pl.pallas_call(
        functools.partial(_back_kernel, ff_chunk=1024),
        grid=(R // tm,),
        in_specs=[row(D_MODEL), row(C_CONV), row(ATTN_DIM), const((D_MODEL, D_MODEL)),
                  const((1, D_MODEL)), const((1, D_MODEL)), const((1, D_MODEL)),
                  const((D_MODEL, D_FF)), const((D_FF, D_MODEL))],
        out_specs=row(D_MODEL), out_shape=jax.ShapeDtypeStruct((R, D_MODEL), F32),
        compiler_params=_cparams(1), name="back",
    )(x2, conv2, attn2, w_out_b, g_post, g_mpre, g_mpost, w_up_b, w_dn_b)


def _front_sample_kernel(x_ref, g_ref, w_ref, cos_ref, sin_ref, wdw_ref, bdw_ref, lng_ref, lnb_ref, st_ref,
                         k_out, v_out, ki_out, q_out, qi_out, w_out, conv_out, state_out, ext_ref, *, gs, ds):
    a, gate, q, k, v, qi, ki_slab, w_eff = _project(x_ref[...], g_ref[...], w_ref, cos_ref[...], sin_ref[...])
    k_out[...] = k
    v_out[...] = v
    ki_out[...] = ki_slab[:, :D_IDX]
    q_out[...] = q.astype(BF16)
    qi_out[...] = qi.astype(BF16)
    w_out[...] = w_eff

    ext_ref[:, 0:CONV_HALO, :] = st_ref[...]
    ext_ref[:, CONV_HALO:CONV_HALO + ds, :] = _glu(a, gate).reshape(gs, ds, C_CONV)
    acc = jnp.zeros((gs, ds, C_CONV), F32)
    for t in range(CONV_WIDTH):
        acc = acc + wdw_ref[t:t + 1, :] * ext_ref[:, t:t + ds, :]
    y = _ln_swish(acc.reshape(gs * ds, C_CONV) + bdw_ref[...], lng_ref[...], lnb_ref[...])
    conv_out[...] = y.astype(BF16)
    state_out[...] = ext_ref[:, ds:ds + CONV_HALO, :]


def _front_sample(x2, g_pre, w_in_b, cos, sin_signed, w_dw, b_dw, ln_g, ln_b, state, *, gs, ds):
    R = x2.shape[0]
    DB = R // ds
    tm = gs * ds
    row = lambda w: pl.BlockSpec((tm, w), lambda r: (r, 0))
    const = lambda shp: pl.BlockSpec(shp, lambda r: (0,) * len(shp))
    st_spec = pl.BlockSpec((gs, CONV_HALO, C_CONV), lambda r: (r, 0, 0))
    out_shape = (
        jax.ShapeDtypeStruct((R, ATTN_DIM), F32), jax.ShapeDtypeStruct((R, ATTN_DIM), F32),
        jax.ShapeDtypeStruct((R, D_IDX), F32),
        jax.ShapeDtypeStruct((R, ATTN_DIM), BF16), jax.ShapeDtypeStruct((R, H_IDX * D_IDX), BF16),
        jax.ShapeDtypeStruct((R, LANES), F32),
        jax.ShapeDtypeStruct((R, C_CONV), BF16), jax.ShapeDtypeStruct((DB, CONV_HALO, C_CONV), F32))
    out_specs = (row(ATTN_DIM), row(ATTN_DIM), row(D_IDX), row(ATTN_DIM), row(H_IDX * D_IDX), row(LANES),
                 row(C_CONV), st_spec)
    return pl.pallas_call(
        functools.partial(_front_sample_kernel, gs=gs, ds=ds),
        grid=(R // tm,),
        in_specs=[row(D_MODEL), const((1, D_MODEL)), const((D_MODEL, D_IN_PAD)),
                  const((tm, LANES)), const((tm, LANES)),
                  const((CONV_WIDTH, C_CONV)), const((1, C_CONV)), const((1, C_CONV)), const((1, C_CONV)),
                  st_spec],
        out_specs=out_specs, out_shape=out_shape,
        scratch_shapes=[pltpu.VMEM((gs, CONV_HALO + ds + 2, C_CONV), F32)],
        compiler_params=_cparams(1), name="front_sample",
    )(x2, g_pre, w_in_b, cos, sin_signed, w_dw, b_dw, ln_g, ln_b, state)


def _idx_sample_kernel(pt_ref, *refs, pg, n_pages, ds, gb, topk, idx_bits):
    page_refs = refs[:pg]
    qi_ref, w_ref, kin_ref, sc_ref, thr_ref, wb_ref = refs[pg:]
    s_id = pl.program_id(1)
    bb = pl.program_id(0) % gb
    n_steps = n_pages // pg
    lane = lax.broadcasted_iota(I32, (ds, PAGE_SIZE), 1)
    trow = lax.broadcasted_iota(I32, (ds, PAGE_SIZE), 0)

    @pl.when(s_id == 0)
    def _():
        for h in range(H_IDX):
            wb_ref[h] = jnp.broadcast_to(w_ref[0, :, h:h + 1], (ds, PAGE_SIZE))

    def scores(keys_b, keys_on_lanes):
        contract = (((1,), (0,)), ((), ())) if keys_on_lanes else (((1,), (1,)), ((), ()))
        d = lax.dot_general(qi_ref[0], keys_b, contract, preferred_element_type=F32)
        s = jnp.zeros((ds, PAGE_SIZE), F32)
        for h in range(H_IDX):
            s = s + wb_ref[h] * jnp.maximum(d[h * ds:(h + 1) * ds], 0.0)
        return s

    pages_per_dot = _pick(pg, (8, 4, 2, 1))
    for p0 in range(0, pg, pages_per_dot):
        keys = jnp.concatenate([page_refs[p0 + u][0].astype(BF16) for u in range(pages_per_dot)], axis=1)
        d = jnp.dot(qi_ref[0], keys, preferred_element_type=F32)
        for u in range(pages_per_dot):
            s = jnp.zeros((ds, PAGE_SIZE), F32)
            for h in range(H_IDX):
                s = s + wb_ref[h] * jnp.maximum(d[h * ds:(h + 1) * ds, u * PAGE_SIZE:(u + 1) * PAGE_SIZE], 0.0)
            sc_ref[bb, s_id * pg + p0 + u] = s

    @pl.when(s_id == n_steps - 1)
    def _():
        kin = jnp.concatenate([kin_ref[0], jnp.zeros((PAGE_SIZE - ds, D_IDX), BF16)], axis=0)
        sc_ref[bb, n_pages] = jnp.where(lane <= trow, scores(kin, False), -jnp.inf)

    @pl.when((s_id == n_steps - 1) & (bb == gb - 1))
    def _():
        shape4 = (gb, n_pages + 1, ds, PAGE_SIZE)

        def count(pred):
            hits = jnp.where(pred(sc_ref[...]), 1.0, 0.0)
            return jnp.sum(jnp.sum(hits, axis=1, keepdims=True), axis=3, keepdims=True)

        shape = (gb, 1, ds, 1)
        thr, n_ge = _bisect_threshold(lambda t: count(lambda s: s >= t), jnp.full(shape, -F32_INF_BITS, I32),
                                      jnp.full(shape, F32_INF_BITS, I32), topk)
        tied = (n_ge > topk) & (thr > -jnp.inf)

        @pl.when(jnp.max(jnp.where(tied, 1.0, 0.0)) > 0.0)
        def _():
            key_idx = (lax.broadcasted_iota(I32, shape4, 1) * PAGE_SIZE
                       + lax.broadcasted_iota(I32, shape4, 3))
            need = topk - count(lambda s: s > thr)
            last = _bisect_tie_index(lambda cand: count(lambda s: (s == thr) & (key_idx < cand)),
                                     need, (gb, 1, ds, 1), idx_bits)
            s = sc_ref[...]
            sc_ref[...] = jnp.where(tied & (s == thr) & (key_idx > last), -jnp.inf, s)

        thr_ref[...] = jnp.broadcast_to(thr.reshape(gb, ds, 1), (gb, ds, PAGE_SIZE))


def _idx_sample(page_table_flat, kidx_pages, qi_st, w_s, ki_new_b, *, pg, n_pages, ds):
    DB = qi_st.shape[0]
    gb = _pick(DB, (16, 8, 4, 2, 1))
    L = n_pages * PAGE_SIZE + ds
    topk = min(TOPK_MAX, L // 4)
    page_spec = lambda p: pl.BlockSpec((1, D_IDX, PAGE_SIZE),
                                       lambda b, s, pt: (pt[b * n_pages + s * pg + p], 0, 0))
    per_seq = lambda shp: pl.BlockSpec((1,) + shp, lambda b, s, pt: (b,) + (0,) * len(shp))
    per_group = lambda shp: pl.BlockSpec((gb,) + shp, lambda b, s, pt: (b // gb,) + (0,) * len(shp))
    grid_spec = pltpu.PrefetchScalarGridSpec(
        num_scalar_prefetch=1, grid=(DB, n_pages // pg),
        in_specs=[page_spec(p) for p in range(pg)]
        + [per_seq((H_IDX * ds, D_IDX)), per_seq((ds, H_IDX)), per_seq((ds, D_IDX))],
        out_specs=[per_group((n_pages + 1, ds, PAGE_SIZE)), per_group((ds, PAGE_SIZE))],
        scratch_shapes=[pltpu.VMEM((H_IDX, ds, PAGE_SIZE), F32)])
    return pl.pallas_call(
        functools.partial(_idx_sample_kernel, pg=pg, n_pages=n_pages, ds=ds, gb=gb, topk=topk,
                          idx_bits=int(L).bit_length()),
        grid_spec=grid_spec,
        out_shape=(jax.ShapeDtypeStruct((DB, n_pages + 1, ds, PAGE_SIZE), F32),
                   jax.ShapeDtypeStruct((DB, ds, PAGE_SIZE), F32)),
        compiler_params=_cparams(2), name="idx_sample",
    )(page_table_flat, *([kidx_pages] * pg), qi_st, w_s, ki_new_b)


def _attn_sample_kernel(pt_ref, *refs, pg, n_pages, ds):
    k_refs, v_refs = refs[:pg], refs[pg:2 * pg]
    q_ref, kn_ref, vn_ref, sc_ref, thr_ref, out_ref, qbd_ref, acc_ref, m_ref, l_ref = refs[2 * pg:]
    s_id = pl.program_id(1)
    n_steps = n_pages // pg
    hq = N_HEADS * ds
    head_of_row = lax.broadcasted_iota(I32, (hq, ATTN_DIM), 0) // ds
    head_of_col = lax.broadcasted_iota(I32, (hq, ATTN_DIM), 1) // HEAD_DIM

    @pl.when(s_id == 0)
    def _():
        q_rows = jnp.concatenate([q_ref[0]] * N_HEADS, axis=0)
        qbd_ref[...] = jnp.where(head_of_row == head_of_col, q_rows, jnp.zeros_like(q_rows))
        m_ref[...] = jnp.full((hq, 1), NEG_BIG, F32)
        l_ref[...] = jnp.zeros((hq, 1), F32)
        acc_ref[...] = jnp.zeros((hq, ATTN_DIM), F32)

    nn, nt = (((1,), (0,)), ((), ())), (((1,), (1,)), ((), ()))

    def attend(kbs, vbs, sels, keys_on_lanes):
        ss = [lax.dot_general(qbd_ref[...], kb, nn if keys_on_lanes else nt, preferred_element_type=F32)
              + jnp.concatenate([jnp.where(sel, 0.0, -jnp.inf)] * N_HEADS, axis=0)
              for kb, sel in zip(kbs, sels)]
        blk_max = functools.reduce(jnp.maximum, ss)
        m_old = m_ref[...]
        m_new = jnp.maximum(m_old, jnp.max(blk_max, axis=1, keepdims=True))
        alpha = jnp.exp2(m_old - m_new)
        ps = [jnp.exp2(s - m_new) for s in ss]
        l_ref[...] = alpha * l_ref[...] + jnp.sum(functools.reduce(jnp.add, ps), axis=1, keepdims=True)
        m_ref[...] = m_new
        pv = functools.reduce(jnp.add, [
            lax.dot_general(p.astype(BF16), vb, nt if keys_on_lanes else nn, preferred_element_type=F32)
            for p, vb in zip(ps, vbs)])
        acc_ref[...] = alpha * acc_ref[...] + pv

    thr = thr_ref[0]
    attend([r[0].astype(BF16) for r in k_refs], [r[0].astype(BF16) for r in v_refs],
           [sc_ref[0, s_id * pg + p] >= thr for p in range(pg)], True)

    @pl.when(s_id == n_steps - 1)
    def _():
        pad = jnp.zeros((PAGE_SIZE - ds, ATTN_DIM), BF16)
        lane = lax.broadcasted_iota(I32, (ds, PAGE_SIZE), 1)
        trow = lax.broadcasted_iota(I32, (ds, PAGE_SIZE), 0)
        sel = (sc_ref[0, n_pages] >= thr) & (lane <= trow)
        attend([jnp.concatenate([kn_ref[0], pad], axis=0)], [jnp.concatenate([vn_ref[0], pad], axis=0)],
               [sel], False)
        o = jnp.where(head_of_row == head_of_col, acc_ref[...] / l_ref[...], 0.0)
        out = o[0:ds]
        for h in range(1, N_HEADS):
            out = out + o[h * ds:(h + 1) * ds]
        out_ref[0] = out.astype(BF16)


def _attn_sample(page_table_flat, k_pages, v_pages, q_s, k_new_b, v_new_b, scores, thr, *, pg, n_pages, ds):
    DB = q_s.shape[0]
    page_spec = lambda p: pl.BlockSpec((1, ATTN_DIM, PAGE_SIZE),
                                       lambda b, s, pt: (pt[b * n_pages + s * pg + p], 0, 0))
    per_seq = lambda shp: pl.BlockSpec((1,) + shp, lambda b, s, pt: (b,) + (0,) * len(shp))
    hq = N_HEADS * ds
    grid_spec = pltpu.PrefetchScalarGridSpec(
        num_scalar_prefetch=1, grid=(DB, n_pages // pg),
        in_specs=[page_spec(p) for p in range(pg)] * 2
        + [per_seq((ds, ATTN_DIM))] * 3 + [per_seq((n_pages + 1, ds, PAGE_SIZE)), per_seq((ds, PAGE_SIZE))],
        out_specs=per_seq((ds, ATTN_DIM)),
        scratch_shapes=[pltpu.VMEM((hq, ATTN_DIM), BF16), pltpu.VMEM((hq, ATTN_DIM), F32),
                        pltpu.VMEM((hq, 1), F32), pltpu.VMEM((hq, 1), F32)])
    return pl.pallas_call(
        functools.partial(_attn_sample_kernel, pg=pg, n_pages=n_pages, ds=ds),
        grid_spec=grid_spec, out_shape=jax.ShapeDtypeStruct((DB, ds, ATTN_DIM), BF16),
        compiler_params=_cparams(2), name="attn_sample",
    )(page_table_flat, *([k_pages] * pg), *([v_pages] * pg), q_s, k_new_b, v_new_b, scores, thr)


def _rope_tables(pos):
    half = HEAD_DIM // 2
    inv = ROPE_THETA ** (-jnp.arange(half, dtype=F32) / half)
    ang = pos.astype(F32)[:, None] * inv[None, :]
    cos, sin = jnp.cos(ang), jnp.sin(ang)
    return jnp.tile(cos, (1, 4)), jnp.tile(jnp.concatenate([-sin, sin], axis=1), (1, 2))


def _pick(n, prefs):
    for p in prefs:
        if n % p == 0:
            return p
    raise ValueError(f"no supported tile for extent {n}")


def kernel(x_prompt, x_sample, cache_k, cache_v, cache_kidx, state_conv, page_table, norm_attn_pre,
           norm_attn_post, w_in, w_dw, b_dw, conv_ln_g, conv_ln_b, w_out, norm_mlp_pre, norm_mlp_post,
           w_up, w_down):
    B, S, _ = x_prompt.shape
    DB, DS, _ = x_sample.shape
    n_pages = page_table.shape[1]
    past_len = n_pages * PAGE_SIZE
    assert w_in.shape[0] == 1, "single-layer kernel"
    assert DS == SUBLANES, "sample rows per sequence must fill one sublane tile"

    w_in_b = jnp.pad(w_in[0], ((0, 0), (0, D_IN_PAD - D_IN))).astype(BF16)
    w_out_b, w_up_b, w_dn_b = w_out[0].astype(BF16), w_up[0].astype(BF16), w_down[0].astype(BF16)
    g_pre, g_post = norm_attn_pre, norm_attn_post
    g_mpre, g_mpost = norm_mlp_pre, norm_mlp_post

    tm = _pick(S, (256, 128))
    tk = _pick(S, (256, 128))
    tq = _pick(S, (256, 128))
    cos_p, sin_p = _rope_tables(jnp.arange(S, dtype=I32))
    assert tm == tq, "the front kernel lays qi out per attention query tile"
    (kT_p, vT_p, kiT_p, k4, vT4, qT, qi2, wT, ki4, kn2, conv_p, state_p) = _front_prompt(
        x_prompt, g_pre, w_in_b, cos_p, sin_p, w_dw[0], b_dw, conv_ln_g, conv_ln_b, tm=tm, tk=tk)
    attn_p = _attn_prompt(qT, qi2, wT, k4, vT4, ki4, kn2, tq=tq, tk=tk)
    y_p = _back(x_prompt.reshape(B * S, D_MODEL), conv_p.reshape(B * S, C_CONV),
                attn_p.reshape(B * S, ATTN_DIM), w_out_b, g_post, g_mpre, g_mpost, w_up_b, w_dn_b,
                tm=tm).reshape(B, S, D_MODEL)

    R = DB * DS
    gs = _pick(DB, (32, 16, 8, 4, 2, 1))
    cos_s, sin_s = _rope_tables(past_len + jnp.arange(DS, dtype=I32))
    cos_s, sin_s = jnp.tile(cos_s, (gs, 1)), jnp.tile(sin_s, (gs, 1))
    (k_s, v_s, ki_s, q_s, qi_s, w_s, conv_s, state_s) = _front_sample(
        x_sample.reshape(R, D_MODEL), g_pre, w_in_b, cos_s, sin_s, w_dw[0], b_dw, conv_ln_g, conv_ln_b,
        state_conv[0], gs=gs, ds=DS)
    pg = _pick(n_pages, (32, 16, 8, 4, 2, 1))
    pt_flat = page_table.reshape(-1)
    qi_st = qi_s.reshape(DB, DS, H_IDX, D_IDX).transpose(0, 2, 1, 3).reshape(DB, H_IDX * DS, D_IDX)
    w_sq = w_s[:, D_IDX:D_IDX + H_IDX].reshape(DB, DS, H_IDX)
    n_phys = cache_k.shape[1]
    pagesT = lambda c: jnp.transpose(c[0], (0, 2, 3, 1)).reshape(n_phys, ATTN_DIM, PAGE_SIZE)
    scores, thr = _idx_sample(pt_flat, jnp.transpose(cache_kidx[0], (0, 2, 1)), qi_st, w_sq,
                              ki_s.astype(BF16).reshape(DB, DS, D_IDX),
                              pg=_pick(n_pages, (64, 32, 16, 8, 4, 2, 1)), n_pages=n_pages, ds=DS)
    attn_s = _attn_sample(pt_flat, pagesT(cache_k), pagesT(cache_v),
                          q_s.reshape(DB, DS, ATTN_DIM), k_s.astype(BF16).reshape(DB, DS, ATTN_DIM),
                          v_s.astype(BF16).reshape(DB, DS, ATTN_DIM), scores, thr,
                          pg=pg, n_pages=n_pages, ds=DS)
    y_s = _back(x_sample.reshape(R, D_MODEL), conv_s, attn_s.reshape(R, ATTN_DIM), w_out_b, g_post, g_mpre,
                g_mpost, w_up_b, w_dn_b, tm=_pick(R, (256, 128, 64, 32, 16, 8))).reshape(DB, DS, D_MODEL)

    hd = (N_HEADS, HEAD_DIM)
    unT = lambda t: jnp.transpose(t.reshape(B, N_HEADS, HEAD_DIM, S), (0, 3, 1, 2))[None]
    return (y_p, y_s,
            unT(kT_p), unT(vT_p), jnp.transpose(kiT_p, (0, 2, 1))[None], state_p[None],
            k_s.reshape(1, DB, DS, *hd), v_s.reshape(1, DB, DS, *hd), ki_s.reshape(1, DB, DS, D_IDX),
            state_s[None])
```

```python
import functools

import jax
import jax.numpy as jnp
import numpy as np
from jax import lax
from jax.experimental import pallas as pl
from jax.experimental.pallas import tpu as pltpu

F32 = jnp.float32
BF16 = jnp.bfloat16
I32 = jnp.int32

D_MODEL = 1024
C_CONV = 512
CONV_WIDTH = 31
CONV_HALO = CONV_WIDTH - 1
N_HEADS = 8
HEAD_DIM = 64
ATTN_DIM = N_HEADS * HEAD_DIM
H_IDX = 8
D_IDX = 64
TOPK_MAX = 256
D_FF = 4 * D_MODEL
ROPE_THETA = 10000.0
EPS = 1e-6
PAGE_SIZE = 128

COL_A, COL_G, COL_Q, COL_K, COL_V, COL_QI, COL_KW = 0, 512, 1024, 1536, 2048, 2560, 3072
D_IN = 3144
LANES = 128
SUBLANES = 8
D_IN_PAD = COL_KW + LANES

INT_MIN = -(2 ** 31)
F32_INF_BITS = 0x7F800000
NEG_BIG = -1e30
BF16_ROWS = 16
V_ROWS = HEAD_DIM + BF16_ROWS
LOG2_E = 1.4426950408889634
MAX_RAW_LOGIT = 64.0
NORM_SLACK = 1.05

VMEM_LIMIT = 56 * 1024 * 1024


def _cparams(n_axes):
    return pltpu.CompilerParams(dimension_semantics=("arbitrary",) * n_axes,
                                vmem_limit_bytes=VMEM_LIMIT)


def _rms(x):
    return x * lax.rsqrt(jnp.mean(x * x, axis=-1, keepdims=True) + EPS)


def _rope_slab(x, cos, sin_signed, first_half):
    outs = []
    for c in range(x.shape[1] // LANES):
        xc = x[:, c * LANES:(c + 1) * LANES]
        partner = jnp.where(first_half, pltpu.roll(xc, LANES - 32, 1), pltpu.roll(xc, 32, 1))
        outs.append(xc * cos + partner * sin_signed)
    return outs[0] if len(outs) == 1 else jnp.concatenate(outs, axis=1)


def _projector(x, g, w_ref, cos, sin_signed):
    hb = (_rms(x) * g).astype(BF16)
    dot = lambda c0, n: jnp.dot(hb, w_ref[:, c0:c0 + n], preferred_element_type=F32)
    first_half = (lax.broadcasted_iota(I32, (x.shape[0], LANES), 1) % 64) < 32
    rope = lambda z: _rope_slab(z, cos, sin_signed, first_half)

    def kw():
        z = dot(COL_KW, LANES)
        return rope(z), z * ((H_IDX ** -0.5) * (D_IDX ** -0.5))

    return dict(a=lambda: dot(COL_A, C_CONV), gate=lambda: dot(COL_G, C_CONV),
                q=lambda: rope(dot(COL_Q, ATTN_DIM)) * (HEAD_DIM ** -0.5 * LOG2_E),
                k=lambda: rope(dot(COL_K, ATTN_DIM)), v=lambda: dot(COL_V, ATTN_DIM),
                qi=lambda: rope(dot(COL_QI, H_IDX * D_IDX)), kw=kw)


def _project(x, g, w_ref, cos, sin_signed):
    p = _projector(x, g, w_ref, cos, sin_signed)
    a, gate, q, k, v, qi = (p[n]() for n in ("a", "gate", "q", "k", "v", "qi"))
    return (a, gate, q, k, v, qi) + p["kw"]()


def _glu(a, gate):
    return a * (1.0 / (1.0 + jnp.exp(-gate)))


def _ln_swish(y, ln_g, ln_b):
    mu = jnp.mean(y, axis=-1, keepdims=True)
    d = y - mu
    var = jnp.mean(d * d, axis=-1, keepdims=True)
    y = (d * lax.rsqrt(var + EPS)) * ln_g + ln_b
    return y * (1.0 / (1.0 + jnp.exp(-y)))


def _key_to_f32(c):
    c = jnp.maximum(c, -F32_INF_BITS)
    bits = jnp.where(c >= 0, c, INT_MIN - c)
    return lax.bitcast_convert_type(bits, F32)


def _f32_to_key(x):
    bits = lax.bitcast_convert_type(x, I32)
    return jnp.where(bits >= 0, bits, INT_MIN - bits)


def _bisect_threshold(count_ge, lo, hi, topk, unchecked_steps=14):
    def still_open(lo, hi, cnt):
        return (cnt != topk) & (hi > lo + 1)

    def step(lo, hi, cnt):
        act = still_open(lo, hi, cnt)
        mid = jnp.right_shift(lo, 1) + jnp.right_shift(hi, 1) + (lo & hi & 1)
        c = count_ge(_key_to_f32(mid))
        take = act & (c >= topk)
        drop = act & (c < topk)
        return jnp.where(take, mid, lo), jnp.where(drop, mid, hi), jnp.where(take, c, cnt)

    def body(st):
        lo, hi, cnt, _ = st
        lo, hi, cnt = step(*step(lo, hi, cnt))
        return lo, hi, cnt, jnp.max(jnp.where(still_open(lo, hi, cnt), 1.0, 0.0)) > 0.0

    st = lax.fori_loop(0, unchecked_steps // 2, lambda _, st: step(*step(*st)),
                       (lo, hi, jnp.full(lo.shape, jnp.inf, F32)))
    lo, _, cnt, _ = lax.while_loop(lambda st: st[3], body, st + (jnp.bool_(True),))
    return _key_to_f32(lo), cnt


def _bisect_tie_index(count_eq_below, need, shape, nbits):
    def body(b, c):
        cand = c + lax.shift_left(jnp.int32(1), jnp.int32(nbits - 1) - b)
        return jnp.where(count_eq_below(cand) < need, cand, c)
    return lax.fori_loop(0, nbits, body, jnp.zeros(shape, I32))


def _front_prompt_kernel(x_ref, g_ref, w_ref, cos_ref, sin_ref, wdw_ref, bdw_ref, lng_ref, lnb_ref,
                         kT_out, vT_out, kiT_out, kb_out, vTb_out, qT_out, qi2_out, wT_out, kib_out, kn2_out,
                         conv_out, state_out, ext_ref, sh_ref, *, tm, tk, conv_rows):
    j = pl.program_id(1)
    proj = _projector(x_ref[0], g_ref[...], w_ref, cos_ref[...], sin_ref[...])

    cur = lax.bitwise_and(j, 1)
    prev = 1 - cur

    @pl.when(j == 0)
    def _():
        ext_ref[1] = jnp.zeros((tm + 32, C_CONV), F32)
    ext_prev = ext_ref.at[prev]
    for r in range(1, SUBLANES):
        sh_ref[r - 1] = ext_prev[r:r + tm + 24, :]

    def conv_rows_from(r0):
        acc = jnp.zeros((conv_rows, C_CONV), F32)
        for t in range(CONV_WIDTH):
            lo = r0 + 32 - CONV_HALO + t
            r = lo % SUBLANES
            src = ext_prev if r == 0 else sh_ref.at[r - 1]
            acc = acc + wdw_ref[t:t + 1, :] * src[lo - r:lo - r + conv_rows, :]
        y = _ln_swish(acc + bdw_ref[...], lng_ref[...], lnb_ref[...])
        conv_out[0, r0:r0 + conv_rows, :] = y.astype(BF16)
    conv_chunks = iter(range(0, tm, conv_rows))

    def conv_step():
        r0 = next(conv_chunks, None)
        if r0 is not None:
            conv_rows_from(r0)

    k = proj["k"]()
    kT = k.T
    kT_out[0] = kT
    kn2 = jnp.max(jnp.sum((kT * kT).reshape(N_HEADS, HEAD_DIM, tm), axis=1), axis=1, keepdims=True)
    kn2_out[0, 0] = jnp.broadcast_to(kn2, (N_HEADS, LANES))
    kb = k.astype(BF16)
    for c in range(tm // tk):
        kb_out[0, c] = kb[c * tk:(c + 1) * tk]
    conv_step()

    vT = proj["v"]().T
    vT_out[0] = vT
    vTb = vT.astype(BF16)
    ones = jnp.ones((V_ROWS - HEAD_DIM, tm), BF16)
    vTx = jnp.concatenate([blk for h in range(N_HEADS)
                           for blk in (vTb[h * HEAD_DIM:(h + 1) * HEAD_DIM], ones)], axis=0)
    for c in range(tm // tk):
        vTb_out[0, c] = vTx[:, c * tk:(c + 1) * tk]
    conv_step()

    qT_out[0] = proj["q"]().T.astype(BF16)
    conv_step()

    qiT = proj["qi"]().T.astype(BF16)
    qi2_out[0, 0] = jnp.concatenate([qiT[h * D_IDX:(h + 1) * D_IDX] for h in range(H_IDX)], axis=1)
    conv_step()

    ki_slab, w_eff = proj["kw"]()
    kiT_out[0] = ki_slab.T[:D_IDX]
    kib = ki_slab[:, :D_IDX].astype(BF16)
    for c in range(tm // tk):
        kib_out[0, c] = kib[c * tk:(c + 1) * tk]
    wT_out[0] = w_eff.T[D_IDX:D_IDX + H_IDX]
    for _ in conv_chunks:
        conv_rows_from(_)

    state_out[0] = ext_prev[32 + tm - CONV_HALO:32 + tm, :]
    ext_ref[cur, 0:32, :] = ext_prev[tm:tm + 32, :]
    ext_ref[cur, 32:32 + tm, :] = _glu(proj["a"](), proj["gate"]())


def _front_prompt(x, g_pre, w_in_b, cos, sin_signed, w_dw, b_dw, ln_g, ln_b, *, tm, tk):
    B, S, _ = x.shape
    n_c = S // tk
    nj = S // tm
    t = lambda j: jnp.minimum(j, nj - 1)
    row = lambda w: pl.BlockSpec((1, tm, w), lambda b, j: (b, t(j), 0))
    const = lambda shp: pl.BlockSpec(shp, lambda b, j: (0,) * len(shp))
    chunked = lambda d2, d3: pl.BlockSpec((1, tm // tk, d2, d3), lambda b, j: (b, t(j), 0, 0))
    colT = lambda r: pl.BlockSpec((1, r, tm), lambda b, j: (b, 0, t(j)))
    out_shape = (
        jax.ShapeDtypeStruct((B, ATTN_DIM, S), F32),
        jax.ShapeDtypeStruct((B, ATTN_DIM, S), F32),
        jax.ShapeDtypeStruct((B, D_IDX, S), F32),
        jax.ShapeDtypeStruct((B, n_c, tk, ATTN_DIM), BF16),
        jax.ShapeDtypeStruct((B, n_c, N_HEADS * V_ROWS, tk), BF16),
        jax.ShapeDtypeStruct((B, ATTN_DIM, S), BF16),
        jax.ShapeDtypeStruct((B, S // tm, D_IDX, H_IDX * tm), BF16),
        jax.ShapeDtypeStruct((B, H_IDX, S), F32),
        jax.ShapeDtypeStruct((B, n_c, tk, D_IDX), BF16),
        jax.ShapeDtypeStruct((B, S // tm, N_HEADS, LANES), F32),
        jax.ShapeDtypeStruct((B, S, C_CONV), BF16),
        jax.ShapeDtypeStruct((B, CONV_HALO, C_CONV), F32),
    )
    out_specs = (colT(ATTN_DIM), colT(ATTN_DIM), colT(D_IDX), chunked(tk, ATTN_DIM),
                 chunked(N_HEADS * V_ROWS, tk), colT(ATTN_DIM),
                 pl.BlockSpec((1, 1, D_IDX, H_IDX * tm), lambda b, j: (b, t(j), 0, 0)),
                 colT(H_IDX), chunked(tk, D_IDX),
                 pl.BlockSpec((1, 1, N_HEADS, LANES), lambda b, j: (b, t(j), 0, 0)),
                 pl.BlockSpec((1, tm, C_CONV), lambda b, j: (b, jnp.maximum(j - 1, 0), 0)),
                 pl.BlockSpec((1, CONV_HALO, C_CONV), lambda b, j: (b, 0, 0)))
    in_specs = [row(D_MODEL), const((1, D_MODEL)), const((D_MODEL, D_IN_PAD)),
                pl.BlockSpec((tm, LANES), lambda b, j: (t(j), 0)), pl.BlockSpec((tm, LANES), lambda b, j: (t(j), 0)),
                const((CONV_WIDTH, C_CONV)), const((1, C_CONV)), const((1, C_CONV)), const((1, C_CONV))]
    return pl.pallas_call(
        functools.partial(_front_prompt_kernel, tm=tm, tk=tk, conv_rows=64),
        grid=(B, nj + 1), in_specs=in_specs, out_specs=out_specs, out_shape=out_shape,
        scratch_shapes=[pltpu.VMEM((2, tm + 32, C_CONV), F32),
                        pltpu.VMEM((SUBLANES - 1, tm + 24, C_CONV), F32)],
        compiler_params=_cparams(2), name="front_prompt",
    )(x, g_pre, w_in_b, cos, sin_signed, w_dw, b_dw, ln_g, ln_b)


def _attn_prompt_kernel(qT_ref, qi2_ref, wT_ref, k_ref, vT_ref, ki_ref, kn2_ref, out_ref,
                        sc_ref, gm_ref, qz_ref, acc_ref, m_ref, *, tq, tk, topk, idx_bits):
    i = pl.program_id(1)
    q0 = i * tq
    n_diag = tq // tk
    n_full = i * n_diag
    n_chunks = n_full + n_diag
    n_rg = tk // SUBLANES
    row_iota = lax.broadcasted_iota(I32, (tk, tq), 0)
    qpos = q0 + lax.broadcasted_iota(I32, (tk, tq), 1)

    qf = qT_ref[0].astype(F32)
    qn2 = jnp.sum((qf * qf).reshape(N_HEADS, HEAD_DIM, tq), axis=1)
    kn2 = jnp.max(kn2_ref[0], axis=0)[:, 0:1]
    small_logits = jnp.max(qn2 * kn2) * NORM_SLACK <= MAX_RAW_LOGIT ** 2

    def over_chunks(stage1, stage2):
        def pair(g, carry):
            a0, a1 = stage1(2 * g, False), stage1(2 * g + 1, False)
            stage2(2 * g, a0, False)
            stage2(2 * g + 1, a1, False)
            return carry
        lax.fori_loop(0, lax.shift_right_logical(n_full, 1), pair, 0)

        @pl.when(lax.bitwise_and(n_full, 1) == 1)
        def _():
            stage2(n_full - 1, stage1(n_full - 1, False), False)
        for dch in range(n_diag):
            stage2(n_full + dch, stage1(n_full + dch, True), True)

    tm = qi2_ref.shape[3] // H_IDX
    def score_dots(c, causal):
        return [jnp.dot(ki_ref[0, c], qi2_ref[0, u], preferred_element_type=F32) for u in range(tq // tm)]

    def score_reduce(c, ds_, causal):
        parts = []
        for u, d in enumerate(ds_):
            s = jnp.zeros((tk, tm), F32)
            for h in range(H_IDX):
                s = s + wT_ref[0, h:h + 1, u * tm:(u + 1) * tm] * jnp.maximum(d[:, h * tm:(h + 1) * tm], 0.0)
            parts.append(s)
        s = parts[0] if len(parts) == 1 else jnp.concatenate(parts, axis=1)
        if causal:
            s = jnp.where(c * tk + row_iota <= qpos, s, -jnp.inf)
        sc_ref[c] = s
        gm_ref[...] = jnp.maximum(gm_ref[...], s)

    gm_ref[...] = jnp.full((tk, tq), -jnp.inf, F32)
    over_chunks(score_dots, score_reduce)

    @pl.when(lax.bitwise_and(n_chunks, 1) == 1)
    def _():
        sc_ref[n_chunks] = jnp.full((tk, tq), -jnp.inf, F32)

    def count(pred):
        def body(g, acc):
            for u in range(2):
                c = 2 * g + u
                m = jnp.where(pred(c, sc_ref[c]), 1.0, 0.0).reshape(n_rg, SUBLANES, tq)
                acc = acc + jnp.sum(m, axis=0)
            return acc
        acc = lax.fori_loop(0, lax.shift_right_logical(n_chunks + 1, 1), body, jnp.zeros((SUBLANES, tq), F32))
        return jnp.sum(acc, axis=0, keepdims=True)

    gm = gm_ref[...]
    hi_key = _f32_to_key(jnp.max(gm, axis=0, keepdims=True)) + 1
    lo_val = jnp.min(gm, axis=0, keepdims=True) if tk >= topk else jnp.full((1, tq), -jnp.inf, F32)
    thr, n_ge = _bisect_threshold(lambda t: count(lambda c, s: s >= t), _f32_to_key(lo_val), hi_key, topk)
    unknown = (n_ge == jnp.inf) & (thr > -jnp.inf)
    n_ge = lax.cond(jnp.max(jnp.where(unknown, 1.0, 0.0)) > 0.0,
                    lambda: jnp.where(unknown, count(lambda c, s: s >= thr), n_ge), lambda: n_ge)
    tied = (n_ge > topk) & (thr > -jnp.inf)

    @pl.when(jnp.max(jnp.where(tied, 1.0, 0.0)) > 0.0)
    def _():
        need = topk - count(lambda c, s: s > thr)
        last = _bisect_tie_index(
            lambda cand: count(lambda c, s: (s == thr) & (c * tk + row_iota < cand)),
            need, (1, tq), idx_bits)

        def drop(c, carry):
            s = sc_ref[c]
            sc_ref[c] = jnp.where(tied & (s == thr) & (c * tk + row_iota > last), -jnp.inf, s)
            return carry
        lax.fori_loop(0, n_chunks, drop, 0)

    zeros_half = jnp.zeros((HEAD_DIM, tq), BF16)
    for pr in range(N_HEADS // 2):
        q_even = qT_ref[0, (2 * pr) * HEAD_DIM:(2 * pr + 1) * HEAD_DIM, :]
        q_odd = qT_ref[0, (2 * pr + 1) * HEAD_DIM:(2 * pr + 2) * HEAD_DIM, :]
        qz_ref[pr] = jnp.concatenate([jnp.concatenate([q_even, zeros_half], axis=1),
                                      jnp.concatenate([zeros_half, q_odd], axis=1)], axis=0)
    m_ref[...] = jnp.full((N_HEADS, SUBLANES, tq), NEG_BIG, F32)
    acc_ref[...] = jnp.zeros((N_HEADS * V_ROWS, tq), F32)

    def masked_bias(c, causal):
        sel = sc_ref[c] >= thr
        if causal:
            sel = sel & (c * tk + row_iota <= qpos)
        return jnp.where(sel, 0.0, -jnp.inf)

    def raw_probs(c, causal):
        neg = masked_bias(c, causal)
        probs = []
        for pr in range(N_HEADS // 2):
            d = jnp.dot(k_ref[0, c, :, pr * LANES:(pr + 1) * LANES], qz_ref[pr], preferred_element_type=F32)
            probs += [jnp.exp2(d[:, e * tq:(e + 1) * tq] + neg).astype(BF16) for e in range(2)]
        return probs

    def raw_accumulate(c, probs, causal):
        for h in range(N_HEADS):
            rows = slice(h * V_ROWS, (h + 1) * V_ROWS)
            acc_ref[rows, :] += jnp.dot(vT_ref[0, c, rows, :], probs[h], preferred_element_type=F32)

    def attend_chunk(c, _, causal):
        neg = masked_bias(c, causal)
        masked, col_max = [], []
        for pr in range(N_HEADS // 2):
            d = jnp.dot(k_ref[0, c, :, pr * LANES:(pr + 1) * LANES], qz_ref[pr], preferred_element_type=F32)
            for e in range(2):
                s = (d[:, e * tq:(e + 1) * tq] + neg).reshape(n_rg, SUBLANES, tq)
                masked.append(s)
                col_max.append(jnp.max(jnp.max(s, axis=0), axis=0, keepdims=True))
        for h in range(N_HEADS):
            s = masked[h]
            m_old = m_ref[h]
            m_new = jnp.maximum(m_old, col_max[h])
            alpha = jnp.exp2(m_old - m_new)
            p = jnp.exp2(s - m_new[None]).reshape(tk, tq).astype(BF16)
            m_ref[h] = m_new
            rows = slice(h * V_ROWS, (h + 1) * V_ROWS)
            pv = jnp.dot(vT_ref[0, c, rows, :], p, preferred_element_type=F32)
            acc = acc_ref[rows, :].reshape(V_ROWS // SUBLANES, SUBLANES, tq) * alpha[None]
            acc_ref[rows, :] = acc.reshape(V_ROWS, tq) + pv

    @pl.when(small_logits)
    def _():
        over_chunks(raw_probs, raw_accumulate)

    @pl.when(jnp.logical_not(small_logits))
    def _():
        over_chunks(lambda c, causal: None, attend_chunk)

    outs = []
    for h in range(N_HEADS):
        denom = acc_ref[h * V_ROWS + HEAD_DIM:h * V_ROWS + HEAD_DIM + 1, :]
        outs.append(acc_ref[h * V_ROWS:h * V_ROWS + HEAD_DIM, :] / denom)
    out_ref[0] = jnp.concatenate(outs, axis=0).T.astype(BF16)


def _attn_prompt(qT, qi2, wT, k4, vT4, ki4, kn2, *, tq, tk):
    B, n_c = k4.shape[:2]
    S = n_c * tk
    topk = min(TOPK_MAX, S // 4)
    whole = lambda shp: pl.BlockSpec((1,) + shp, lambda b, i: (b,) + (0,) * len(shp),
                                     pipeline_mode=pl.Buffered(1))
    colT = lambda r: pl.BlockSpec((1, r, tq), lambda b, i: (b, 0, i))
    tm = qi2.shape[3] // H_IDX
    return pl.pallas_call(
        functools.partial(_attn_prompt_kernel, tq=tq, tk=tk, topk=topk, idx_bits=int(S).bit_length()),
        grid=(B, S // tq),
        in_specs=[colT(ATTN_DIM), pl.BlockSpec((1, tq // tm, D_IDX, H_IDX * tm), lambda b, i: (b, i, 0, 0)),
                  colT(H_IDX),
                  whole((n_c, tk, ATTN_DIM)), whole((n_c, N_HEADS * V_ROWS, tk)), whole((n_c, tk, D_IDX)),
                  whole(kn2.shape[1:])],
        out_specs=pl.BlockSpec((1, tq, ATTN_DIM), lambda b, i: (b, i, 0)),
        out_shape=jax.ShapeDtypeStruct((B, S, ATTN_DIM), BF16),
        scratch_shapes=[pltpu.VMEM((n_c + 1, tk, tq), F32),
                        pltpu.VMEM((tk, tq), F32),
                        pltpu.VMEM((N_HEADS // 2, LANES, 2 * tq), BF16),
                        pltpu.VMEM((N_HEADS * V_ROWS, tq), F32),
                        pltpu.VMEM((N_HEADS, SUBLANES, tq), F32)],
        compiler_params=_cparams(2), name="attn_prompt",
    )(qT, qi2, wT, k4, vT4, ki4, kn2)


def _back_kernel(x_ref, c_ref, a_ref, wo_ref, gpost_ref, gmpre_ref, gmpost_ref, wup_ref, wdn_ref, y_ref,
                 *, ff_chunk):
    mix = (jnp.dot(c_ref[...], wo_ref[0:C_CONV, :], preferred_element_type=F32)
           + jnp.dot(a_ref[...], wo_ref[C_CONV:, :], preferred_element_type=F32))
    x1 = x_ref[...] + _rms(mix) * gpost_ref[...]
    hm = (_rms(x1) * gmpre_ref[...]).astype(BF16)
    m = jnp.zeros_like(x1)
    for f in range(0, D_FF, ff_chunk):
        up = jnp.dot(hm, wup_ref[:, f:f + ff_chunk], preferred_element_type=F32)
        act = jnp.square(jnp.maximum(up, 0.0)).astype(BF16)
        m = m + jnp.dot(act, wdn_ref[f:f + ff_chunk, :], preferred_element_type=F32)
    y_ref[...] = x1 + _rms(m) * gmpost_ref[...]


def _back(x2, conv2, attn2, w_out_b, g_post, g_mpre, g_mpost, w_up_b, w_dn_b, *, tm):
    R = x2.shape[0]
    row = lambda w: pl.BlockSpec((tm, w), lambda r: (r, 0))
    const = lambda shp: pl.BlockSpec(shp, lambda r: (0,) * len(shp))
    return pl.pallas_call(
        functools.partial(_back_kernel, ff_chunk=1024),
        grid=(R // tm,),
        in_specs=[row(D_MODEL), row(C_CONV), row(ATTN_DIM), const((D_MODEL, D_MODEL)),
                  const((1, D_MODEL)), const((1, D_MODEL)), const((1, D_MODEL)),
                  const((D_MODEL, D_FF)), const((D_FF, D_MODEL))],
        out_specs=row(D_MODEL), out_shape=jax.ShapeDtypeStruct((R, D_MODEL), F32),
        compiler_params=_cparams(1), name="back",
    )(x2, conv2, attn2, w_out_b, g_post, g_mpre, g_mpost, w_up_b, w_dn_b)


def _front_sample_kernel(x_ref, g_ref, w_ref, cos_ref, sin_ref, wdw_ref, bdw_ref, lng_ref, lnb_ref, st_ref,
                         k_out, v_out, ki_out, q_out, qi_out, w_out, conv_out, state_out, ext_ref, *, gs, ds):
    a, gate, q, k, v, qi, ki_slab, w_eff = _project(x_ref[...], g_ref[...], w_ref, cos_ref[...], sin_ref[...])
    k_out[...] = k
    v_out[...] = v
    ki_out[...] = ki_slab[:, :D_IDX]
    q_out[...] = q.astype(BF16)
    qi_out[...] = qi.astype(BF16)
    w_out[...] = w_eff

    ext_ref[:, 0:CONV_HALO, :] = st_ref[...]
    ext_ref[:, CONV_HALO:CONV_HALO + ds, :] = _glu(a, gate).reshape(gs, ds, C_CONV)
    acc = jnp.zeros((gs, ds, C_CONV), F32)
    for t in range(CONV_WIDTH):
        acc = acc + wdw_ref[t:t + 1, :] * ext_ref[:, t:t + ds, :]
    y = _ln_swish(acc.reshape(gs * ds, C_CONV) + bdw_ref[...], lng_ref[...], lnb_ref[...])
    conv_out[...] = y.astype(BF16)
    state_out[...] = ext_ref[:, ds:ds + CONV_HALO, :]


def _front_sample(x2, g_pre, w_in_b, cos, sin_signed, w_dw, b_dw, ln_g, ln_b, state, *, gs, ds):
    R = x2.shape[0]
    DB = R // ds
    tm = gs * ds
    row = lambda w: pl.BlockSpec((tm, w), lambda r: (r, 0))
    const = lambda shp: pl.BlockSpec(shp, lambda r: (0,) * len(shp))
    st_spec = pl.BlockSpec((gs, CONV_HALO, C_CONV), lambda r: (r, 0, 0))
    out_shape = (
        jax.ShapeDtypeStruct((R, ATTN_DIM), F32), jax.ShapeDtypeStruct((R, ATTN_DIM), F32),
        jax.ShapeDtypeStruct((R, D_IDX), F32),
        jax.ShapeDtypeStruct((R, ATTN_DIM), BF16), jax.ShapeDtypeStruct((R, H_IDX * D_IDX), BF16),
        jax.ShapeDtypeStruct((R, LANES), F32),
        jax.ShapeDtypeStruct((R, C_CONV), BF16), jax.ShapeDtypeStruct((DB, CONV_HALO, C_CONV), F32))
    out_specs = (row(ATTN_DIM), row(ATTN_DIM), row(D_IDX), row(ATTN_DIM), row(H_IDX * D_IDX), row(LANES),
                 row(C_CONV), st_spec)
    return pl.pallas_call(
        functools.partial(_front_sample_kernel, gs=gs, ds=ds),
        grid=(R // tm,),
        in_specs=[row(D_MODEL), const((1, D_MODEL)), const((D_MODEL, D_IN_PAD)),
                  const((tm, LANES)), const((tm, LANES)),
                  const((CONV_WIDTH, C_CONV)), const((1, C_CONV)), const((1, C_CONV)), const((1, C_CONV)),
                  st_spec],
        out_specs=out_specs, out_shape=out_shape,
        scratch_shapes=[pltpu.VMEM((gs, CONV_HALO + ds + 2, C_CONV), F32)],
        compiler_params=_cparams(1), name="front_sample",
    )(x2, g_pre, w_in_b, cos, sin_signed, w_dw, b_dw, ln_g, ln_b, state)


def _idx_sample_kernel(pt_ref, *refs, pg, n_pages, ds, gb, topk, idx_bits):
    page_refs = refs[:pg]
    qi_ref, w_ref, kin_ref, sc_ref, thr_ref, wb_ref = refs[pg:]
    s_id = pl.program_id(1)
    bb = pl.program_id(0) % gb
    n_steps = n_pages // pg
    lane = lax.broadcasted_iota(I32, (ds, PAGE_SIZE), 1)
    trow = lax.broadcasted_iota(I32, (ds, PAGE_SIZE), 0)

    @pl.when(s_id == 0)
    def _():
        for h in range(H_IDX):
            wb_ref[h] = jnp.broadcast_to(w_ref[0, :, h:h + 1], (ds, PAGE_SIZE))

    def scores(keys_b, keys_on_lanes):
        contract = (((1,), (0,)), ((), ())) if keys_on_lanes else (((1,), (1,)), ((), ()))
        d = lax.dot_general(qi_ref[0], keys_b, contract, preferred_element_type=F32)
        s = jnp.zeros((ds, PAGE_SIZE), F32)
        for h in range(H_IDX):
            s = s + wb_ref[h] * jnp.maximum(d[h * ds:(h + 1) * ds], 0.0)
        return s

    pages_per_dot = _pick(pg, (8, 4, 2, 1))
    for p0 in range(0, pg, pages_per_dot):
        keys = jnp.concatenate([page_refs[p0 + u][0].astype(BF16) for u in range(pages_per_dot)], axis=1)
        d = jnp.dot(qi_ref[0], keys, preferred_element_type=F32)
        for u in range(pages_per_dot):
            s = jnp.zeros((ds, PAGE_SIZE), F32)
            for h in range(H_IDX):
                s = s + wb_ref[h] * jnp.maximum(d[h * ds:(h + 1) * ds, u * PAGE_SIZE:(u + 1) * PAGE_SIZE], 0.0)
            sc_ref[bb, s_id * pg + p0 + u] = s

    @pl.when(s_id == n_steps - 1)
    def _():
        kin = jnp.concatenate([kin_ref[0], jnp.zeros((PAGE_SIZE - ds, D_IDX), BF16)], axis=0)
        sc_ref[bb, n_pages] = jnp.where(lane <= trow, scores(kin, False), -jnp.inf)

    @pl.when((s_id == n_steps - 1) & (bb == gb - 1))
    def _():
        shape4 = (gb, n_pages + 1, ds, PAGE_SIZE)

        def count(pred):
            hits = jnp.where(pred(sc_ref[...]), 1.0, 0.0)
            return jnp.sum(jnp.sum(hits, axis=1, keepdims=True), axis=3, keepdims=True)

        shape = (gb, 1, ds, 1)
        thr, n_ge = _bisect_threshold(lambda t: count(lambda s: s >= t), jnp.full(shape, -F32_INF_BITS, I32),
                                      jnp.full(shape, F32_INF_BITS, I32), topk)
        tied = (n_ge > topk) & (thr > -jnp.inf)

        @pl.when(jnp.max(jnp.where(tied, 1.0, 0.0)) > 0.0)
        def _():
            key_idx = (lax.broadcasted_iota(I32, shape4, 1) * PAGE_SIZE
                       + lax.broadcasted_iota(I32, shape4, 3))
            need = topk - count(lambda s: s > thr)
            last = _bisect_tie_index(lambda cand: count(lambda s: (s == thr) & (key_idx < cand)),
                                     need, (gb, 1, ds, 1), idx_bits)
            s = sc_ref[...]
            sc_ref[...] = jnp.where(tied & (s == thr) & (key_idx > last), -jnp.inf, s)

        thr_ref[...] = jnp.broadcast_to(thr.reshape(gb, ds, 1), (gb, ds, PAGE_SIZE))


def _idx_sample(page_table_flat, kidx_pages, qi_st, w_s, ki_new_b, *, pg, n_pages, ds):
    DB = qi_st.shape[0]
    gb = _pick(DB, (16, 8, 4, 2, 1))
    L = n_pages * PAGE_SIZE + ds
    topk = min(TOPK_MAX, L // 4)
    page_spec = lambda p: pl.BlockSpec((1, D_IDX, PAGE_SIZE),
                                       lambda b, s, pt: (pt[b * n_pages + s * pg + p], 0, 0))
    per_seq = lambda shp: pl.BlockSpec((1,) + shp, lambda b, s, pt: (b,) + (0,) * len(shp))
    per_group = lambda shp: pl.BlockSpec((gb,) + shp, lambda b, s, pt: (b // gb,) + (0,) * len(shp))
    grid_spec = pltpu.PrefetchScalarGridSpec(
        num_scalar_prefetch=1, grid=(DB, n_pages // pg),
        in_specs=[page_spec(p) for p in range(pg)]
        + [per_seq((H_IDX * ds, D_IDX)), per_seq((ds, H_IDX)), per_seq((ds, D_IDX))],
        out_specs=[per_group((n_pages + 1, ds, PAGE_SIZE)), per_group((ds, PAGE_SIZE))],
        scratch_shapes=[pltpu.VMEM((H_IDX, ds, PAGE_SIZE), F32)])
    return pl.pallas_call(
        functools.partial(_idx_sample_kernel, pg=pg, n_pages=n_pages, ds=ds, gb=gb, topk=topk,
                          idx_bits=int(L).bit_length()),
        grid_spec=grid_spec,
        out_shape=(jax.ShapeDtypeStruct((DB, n_pages + 1, ds, PAGE_SIZE), F32),
                   jax.ShapeDtypeStruct((DB, ds, PAGE_SIZE), F32)),
        compiler_params=_cparams(2), name="idx_sample",
    )(page_table_flat, *([kidx_pages] * pg), qi_st, w_s, ki_new_b)


def _attn_sample_kernel(pt_ref, *refs, pg, n_pages, ds):
    k_refs, v_refs = refs[:pg], refs[pg:2 * pg]
    q_ref, kn_ref, vn_ref, sc_ref, thr_ref, out_ref, qbd_ref, acc_ref, m_ref, l_ref = refs[2 * pg:]
    s_id = pl.program_id(1)
    n_steps = n_pages // pg
    hq = N_HEADS * ds
    head_of_row = lax.broadcasted_iota(I32, (hq, ATTN_DIM), 0) // ds
    head_of_col = lax.broadcasted_iota(I32, (hq, ATTN_DIM), 1) // HEAD_DIM

    @pl.when(s_id == 0)
    def _():
        q_rows = jnp.concatenate([q_ref[0]] * N_HEADS, axis=0)
        qbd_ref[...] = jnp.where(head_of_row == head_of_col, q_rows, jnp.zeros_like(q_rows))
        m_ref[...] = jnp.full((hq, 1), NEG_BIG, F32)
        l_ref[...] = jnp.zeros((hq, 1), F32)
        acc_ref[...] = jnp.zeros((hq, ATTN_DIM), F32)

    nn, nt = (((1,), (0,)), ((), ())), (((1,), (1,)), ((), ()))

    def attend(kbs, vbs, sels, keys_on_lanes):
        ss = [lax.dot_general(qbd_ref[...], kb, nn if keys_on_lanes else nt, preferred_element_type=F32)
              + jnp.concatenate([jnp.where(sel, 0.0, -jnp.inf)] * N_HEADS, axis=0)
              for kb, sel in zip(kbs, sels)]
        blk_max = functools.reduce(jnp.maximum, ss)
        m_old = m_ref[...]
        m_new = jnp.maximum(m_old, jnp.max(blk_max, axis=1, keepdims=True))
        alpha = jnp.exp2(m_old - m_new)
        ps = [jnp.exp2(s - m_new) for s in ss]
        l_ref[...] = alpha * l_ref[...] + jnp.sum(functools.reduce(jnp.add, ps), axis=1, keepdims=True)
        m_ref[...] = m_new
        pv = functools.reduce(jnp.add, [
            lax.dot_general(p.astype(BF16), vb, nt if keys_on_lanes else nn, preferred_element_type=F32)
            for p, vb in zip(ps, vbs)])
        acc_ref[...] = alpha * acc_ref[...] + pv

    thr = thr_ref[0]
    attend([r[0].astype(BF16) for r in k_refs], [r[0].astype(BF16) for r in v_refs],
           [sc_ref[0, s_id * pg + p] >= thr for p in range(pg)], True)

    @pl.when(s_id == n_steps - 1)
    def _():
        pad = jnp.zeros((PAGE_SIZE - ds, ATTN_DIM), BF16)
        lane = lax.broadcasted_iota(I32, (ds, PAGE_SIZE), 1)
        trow = lax.broadcasted_iota(I32, (ds, PAGE_SIZE), 0)
        sel = (sc_ref[0, n_pages] >= thr) & (lane <= trow)
        attend([jnp.concatenate([kn_ref[0], pad], axis=0)], [jnp.concatenate([vn_ref[0], pad], axis=0)],
               [sel], False)
        o = jnp.where(head_of_row == head_of_col, acc_ref[...] / l_ref[...], 0.0)
        out = o[0:ds]
        for h in range(1, N_HEADS):
            out = out + o[h * ds:(h + 1) * ds]
        out_ref[0] = out.astype(BF16)


def _attn_sample(page_table_flat, k_pages, v_pages, q_s, k_new_b, v_new_b, scores, thr, *, pg, n_pages, ds):
    DB = q_s.shape[0]
    page_spec = lambda p: pl.BlockSpec((1, ATTN_DIM, PAGE_SIZE),
                                       lambda b, s, pt: (pt[b * n_pages + s * pg + p], 0, 0))
    per_seq = lambda shp: pl.BlockSpec((1,) + shp, lambda b, s, pt: (b,) + (0,) * len(shp))
    hq = N_HEADS * ds
    grid_spec = pltpu.PrefetchScalarGridSpec(
        num_scalar_prefetch=1, grid=(DB, n_pages // pg),
        in_specs=[page_spec(p) for p in range(pg)] * 2
        + [per_seq((ds, ATTN_DIM))] * 3 + [per_seq((n_pages + 1, ds, PAGE_SIZE)), per_seq((ds, PAGE_SIZE))],
        out_specs=per_seq((ds, ATTN_DIM)),
        scratch_shapes=[pltpu.VMEM((hq, ATTN_DIM), BF16), pltpu.VMEM((hq, ATTN_DIM), F32),
                        pltpu.VMEM((hq, 1), F32), pltpu.VMEM((hq, 1), F32)])
    return pl.pallas_call(
        functools.partial(_attn_sample_kernel, pg=pg, n_pages=n_pages, ds=ds),
        grid_spec=grid_spec, out_shape=jax.ShapeDtypeStruct((DB, ds, ATTN_DIM), BF16),
        compiler_params=_cparams(2), name="attn_sample",
    )(page_table_flat, *([k_pages] * pg), *([v_pages] * pg), q_s, k_new_b, v_new_b, scores, thr)


def _rope_tables(pos):
    half = HEAD_DIM // 2
    inv = ROPE_THETA ** (-jnp.arange(half, dtype=F32) / half)
    ang = pos.astype(F32)[:, None] * inv[None, :]
    cos, sin = jnp.cos(ang), jnp.sin(ang)
    return jnp.tile(cos, (1, 4)), jnp.tile(jnp.concatenate([-sin, sin], axis=1), (1, 2))


def _pick(n, prefs):
    for p in prefs:
        if n % p == 0:
            return p
    raise ValueError(f"no supported tile for extent {n}")


def kernel(x_prompt, x_sample, cache_k, cache_v, cache_kidx, state_conv, page_table, norm_attn_pre,
           norm_attn_post, w_in, w_dw, b_dw, conv_ln_g, conv_ln_b, w_out, norm_mlp_pre, norm_mlp_post,
           w_up, w_down):
    B, S, _ = x_prompt.shape
    DB, DS, _ = x_sample.shape
    n_pages = page_table.shape[1]
    past_len = n_pages * PAGE_SIZE
    assert w_in.shape[0] == 1, "single-layer kernel"
    assert DS == SUBLANES, "sample rows per sequence must fill one sublane tile"

    w_in_b = jnp.pad(w_in[0], ((0, 0), (0, D_IN_PAD - D_IN))).astype(BF16)
    w_out_b, w_up_b, w_dn_b = w_out[0].astype(BF16), w_up[0].astype(BF16), w_down[0].astype(BF16)
    g_pre, g_post = norm_attn_pre, norm_attn_post
    g_mpre, g_mpost = norm_mlp_pre, norm_mlp_post

    tm = _pick(S, (256, 128))
    tk = _pick(S, (256, 128))
    tq = _pick(S, (512, 256, 128))
    cos_p, sin_p = _rope_tables(jnp.arange(S, dtype=I32))
    assert tq % tm == 0, "an attention query tile spans whole front-kernel row tiles"
    (kT_p, vT_p, kiT_p, k4, vT4, qT, qi2, wT, ki4, kn2, conv_p, state_p) = _front_prompt(
        x_prompt, g_pre, w_in_b, cos_p, sin_p, w_dw[0], b_dw, conv_ln_g, conv_ln_b, tm=tm, tk=tk)
    attn_p = _attn_prompt(qT, qi2, wT, k4, vT4, ki4, kn2, tq=tq, tk=tk)
    y_p = _back(x_prompt.reshape(B * S, D_MODEL), conv_p.reshape(B * S, C_CONV),
                attn_p.reshape(B * S, ATTN_DIM), w_out_b, g_post, g_mpre, g_mpost, w_up_b, w_dn_b,
                tm=tm).reshape(B, S, D_MODEL)

    R = DB * DS
    gs = _pick(DB, (32, 16, 8, 4, 2, 1))
    cos_s, sin_s = _rope_tables(past_len + jnp.arange(DS, dtype=I32))
    cos_s, sin_s = jnp.tile(cos_s, (gs, 1)), jnp.tile(sin_s, (gs, 1))
    (k_s, v_s, ki_s, q_s, qi_s, w_s, conv_s, state_s) = _front_sample(
        x_sample.reshape(R, D_MODEL), g_pre, w_in_b, cos_s, sin_s, w_dw[0], b_dw, conv_ln_g, conv_ln_b,
        state_conv[0], gs=gs, ds=DS)
    pg = _pick(n_pages, (32, 16, 8, 4, 2, 1))
    pt_flat = page_table.reshape(-1)
    qi_st = qi_s.reshape(DB, DS, H_IDX, D_IDX).transpose(0, 2, 1, 3).reshape(DB, H_IDX * DS, D_IDX)
    w_sq = w_s[:, D_IDX:D_IDX + H_IDX].reshape(DB, DS, H_IDX)
    n_phys = cache_k.shape[1]
    pagesT = lambda c: jnp.transpose(c[0], (0, 2, 3, 1)).reshape(n_phys, ATTN_DIM, PAGE_SIZE)
    scores, thr = _idx_sample(pt_flat, jnp.transpose(cache_kidx[0], (0, 2, 1)), qi_st, w_sq,
                              ki_s.astype(BF16).reshape(DB, DS, D_IDX),
                              pg=_pick(n_pages, (64, 32, 16, 8, 4, 2, 1)), n_pages=n_pages, ds=DS)
    attn_s = _attn_sample(pt_flat, pagesT(cache_k), pagesT(cache_v),
                          q_s.reshape(DB, DS, ATTN_DIM), k_s.astype(BF16).reshape(DB, DS, ATTN_DIM),
                          v_s.astype(BF16).reshape(DB, DS, ATTN_DIM), scores, thr,
                          pg=pg, n_pages=n_pages, ds=DS)
    y_s = _back(x_sample.reshape(R, D_MODEL), conv_s, attn_s.reshape(R, ATTN_DIM), w_out_b, g_post, g_mpre,
                g_mpost, w_up_b, w_dn_b, tm=_pick(R, (256, 128, 64, 32, 16, 8))).reshape(DB, DS, D_MODEL)

    hd = (N_HEADS, HEAD_DIM)
    unT = lambda t: jnp.transpose(t.reshape(B, N_HEADS, HEAD_DIM, S), (0, 3, 1, 2))[None]
    return (y_p, y_s,
            unT(kT_p), unT(vT_p), jnp.transpose(kiT_p, (0, 2, 1))[None], state_p[None],
            k_s.reshape(1, DB, DS, *hd), v_s.reshape(1, DB, DS, *hd), ki_s.reshape(1, DB, DS, D_IDX),
            state_s[None])
```
